```python
import jax, jax.numpy as jnp
from jax import lax
import numpy as np

D_MODEL = 1024
BATCH = 8
SEQ = 4096
DEPTH = 2

CTX_LEN = 256
GRID_W = 64
HEAD_DIM = 64
AXIS_DIM = HEAD_DIM // 2
ATTN_WIDTH = D_MODEL // 2
N_Q_HEADS = ATTN_WIDTH // HEAD_DIM
N_KV_HEADS = N_Q_HEADS // 4
KV_WIDTH = N_KV_HEADS * HEAD_DIM
FOURIER_WIDTH = D_MODEL // 4
FOURIER_GROUP = 64
N_FOURIER_GROUPS = FOURIER_WIDTH // FOURIER_GROUP
GMLP_WIDTH = D_MODEL // 4
GMLP_HEAD = 64
N_GMLP_HEADS = GMLP_WIDTH // GMLP_HEAD
CHUNK = 128
Q_BLOCK = 128
MIX_WIDTH = ATTN_WIDTH + FOURIER_WIDTH + GMLP_WIDTH
IN_WIDTH = ATTN_WIDTH + 2 * KV_WIDTH + FOURIER_WIDTH + 2 * GMLP_WIDTH
SPLITS = [ATTN_WIDTH, ATTN_WIDTH + KV_WIDTH, ATTN_WIDTH + 2 * KV_WIDTH,
          ATTN_WIDTH + 2 * KV_WIDTH + FOURIER_WIDTH,
          ATTN_WIDTH + 2 * KV_WIDTH + FOURIER_WIDTH + GMLP_WIDTH]
D_FF = 4 * D_MODEL
ROPE_THETA = 10000.0
EPS = 1e-6

kernel_name = 'hybrid_fourier_gmlp_gqa_prefix_dit'


def rms_norm(x, g):
    xf = x.astype(jnp.float32)
    y = xf * lax.rsqrt(jnp.mean(xf * xf, axis=-1, keepdims=True) + EPS)
    return (y * g.astype(jnp.float32)).astype(x.dtype)


def modulate(h, shift, scale):
    return h * (1 + scale) + shift


def adaln(cond, w, b):
    return jnp.split(jax.nn.silu(cond) @ w + b, 6, axis=-1)


def axial_rope_tables(n, dtype):
    rows = n // GRID_W
    row = jnp.repeat(jnp.arange(rows, dtype=jnp.float32), GRID_W)
    col = jnp.tile(jnp.arange(GRID_W, dtype=jnp.float32), rows)
    inv = ROPE_THETA ** (-jnp.arange(0, AXIS_DIM, 2, dtype=jnp.float32) / AXIS_DIM)
    ar = (row[:, None] * inv)[:, None, :]
    ac = (col[:, None] * inv)[:, None, :]
    return tuple(t.astype(dtype) for t in (jnp.cos(ar), jnp.sin(ar), jnp.cos(ac), jnp.sin(ac)))


def rope_1d(x, cos, sin):
    x1, x2 = jnp.split(x, 2, axis=-1)
    return jnp.concatenate([x1 * cos - x2 * sin, x1 * sin + x2 * cos], axis=-1)


def apply_axial_rope(x, tabs):
    cr, sr, cc, sc = tabs
    xr, xc = jnp.split(x, 2, axis=-1)
    return jnp.concatenate([rope_1d(xr, cr, sr), rope_1d(xc, cc, sc)], axis=-1)


def heads(t, h):
    return t.reshape(t.shape[0], t.shape[1], h, HEAD_DIM)


def attend(q, k, v):
    s = jnp.einsum('bqkgd,btkd->bkgqt', q, k).astype(jnp.float32) * (HEAD_DIM ** -0.5)
    p = jax.nn.softmax(s, axis=-1).astype(v.dtype)
    return jnp.einsum('bkgqt,btkd->bqkgd', p, v)


def latent_attention(q, k, v):
    b, s, hq, d = q.shape
    g = hq // N_KV_HEADS
    nblk = s // Q_BLOCK
    qb = q.reshape(b, nblk, Q_BLOCK, N_KV_HEADS, g, d).transpose(1, 0, 2, 3, 4, 5)
    out = lax.map(lambda qblk: attend(qblk, k, v), qb)
    return out.transpose(1, 0, 2, 3, 4, 5).reshape(b, s, hq * d)


def context_attention(q, k, v):
    b, n, hq, d = q.shape
    g = hq // N_KV_HEADS
    return attend(q.reshape(b, n, N_KV_HEADS, g, d), k, v).reshape(b, n, hq * d)


def fourier_mix(f):
    b, n, _ = f.shape
    fg = f.astype(jnp.float32).reshape(b, n, N_FOURIER_GROUPS, FOURIER_GROUP)
    y = jnp.fft.fft2(fg, axes=(1, 3), norm='ortho').real
    return y.reshape(b, n, FOURIER_WIDTH).astype(f.dtype)


def chunk_spatial_gate(u, v, v_g, w_s, b_s):
    b, n, _ = u.shape
    u = jax.nn.gelu(u)
    vh = jax.nn.gelu(v).reshape(b, n // CHUNK, CHUNK, N_GMLP_HEADS, GMLP_HEAD)
    vh = rms_norm(vh, v_g.reshape(N_GMLP_HEADS, GMLP_HEAD))
    sp = jnp.einsum('hpq,bcqhd->bcphd', w_s, vh) + b_s.T[None, None, :, :, None]
    return u * sp.reshape(b, n, GMLP_WIDTH)


def squared_relu_mlp(h, w1, w2):
    return jnp.square(jax.nn.relu(h @ w1)) @ w2


def layer(x, ctx, c, c_ctx, w_ada, b_ada, g1, g2, w_in, q_g, k_g, v_g, w_s, b_s, w_out, w1, w2, update_ctx):
    sh1, sc1, gt1, sh2, sc2, gt2 = [m[:, None, :] for m in adaln(c, w_ada, b_ada)]
    csh1, csc1, cgt1, csh2, csc2, cgt2 = adaln(c_ctx, w_ada, b_ada)

    px = modulate(rms_norm(x, g1), sh1, sc1) @ w_in
    pc = modulate(rms_norm(ctx, g1), csh1, csc1) @ w_in
    qx, kx, vx, fx, ux, gvx = jnp.split(px, SPLITS, axis=-1)
    qc, kc, vc, fc, uc, gvc = jnp.split(pc, SPLITS, axis=-1)

    tabs = axial_rope_tables(x.shape[1], x.dtype)
    qx = apply_axial_rope(rms_norm(heads(qx, N_Q_HEADS), q_g), tabs)
    kx = apply_axial_rope(rms_norm(heads(kx, N_KV_HEADS), k_g), tabs)
    kc = rms_norm(heads(kc, N_KV_HEADS), k_g)
    vx = heads(vx, N_KV_HEADS)
    vc = heads(vc, N_KV_HEADS)
    k_all = jnp.concatenate([kc, kx], axis=1)
    v_all = jnp.concatenate([vc, vx], axis=1)

    mix_x = jnp.concatenate([latent_attention(qx, k_all, v_all),
                             fourier_mix(fx),
                             chunk_spatial_gate(ux, gvx, v_g, w_s, b_s)], axis=-1) @ w_out
    x = x + gt1 * mix_x
    x = x + gt2 * squared_relu_mlp(modulate(rms_norm(x, g2), sh2, sc2), w1, w2)

    if update_ctx:
        qc = rms_norm(heads(qc, N_Q_HEADS), q_g)
        mix_c = jnp.concatenate([context_attention(qc, kc, vc),
                                 fourier_mix(fc),
                                 chunk_spatial_gate(uc, gvc, v_g, w_s, b_s)], axis=-1) @ w_out
        ctx = ctx + cgt1 * mix_c
        ctx = ctx + cgt2 * squared_relu_mlp(modulate(rms_norm(ctx, g2), csh2, csc2), w1, w2)
    return x, ctx


def setup_inputs(seed: int = 0) -> dict:
    key = jax.random.key(seed)
    ks = jax.random.split(key, 17)
    f32 = jnp.float32
    n = lambda k, s: jax.random.normal(k, s, dtype=f32)
    return {
        'x': n(ks[0], (BATCH, SEQ, D_MODEL)),
        'c': n(ks[1], (BATCH, D_MODEL)),
        'ctx': n(ks[2], (BATCH, CTX_LEN, D_MODEL)),
        'c_ctx': n(ks[3], (D_MODEL,)),
        'w_ada': n(ks[4], (DEPTH, D_MODEL, 6 * D_MODEL)) * (0.5 * D_MODEL ** -0.5),
        'b_ada': n(ks[5], (DEPTH, 6 * D_MODEL)) * 0.01,
        'norm1_g': 1.0 + 0.02 * n(ks[6], (DEPTH, D_MODEL)),
        'norm2_g': 1.0 + 0.02 * n(ks[7], (DEPTH, D_MODEL)),
        'w_in': n(ks[8], (DEPTH, D_MODEL, IN_WIDTH)) * D_MODEL ** -0.5,
        'q_norm_g': 1.0 + 0.02 * n(ks[9], (DEPTH, HEAD_DIM)),
        'k_norm_g': 1.0 + 0.02 * n(ks[10], (DEPTH, HEAD_DIM)),
        'gmlp_v_g': 1.0 + 0.02 * n(ks[11], (DEPTH, GMLP_WIDTH)),
        'w_spatial': n(ks[12], (DEPTH, N_GMLP_HEADS, CHUNK, CHUNK)) * CHUNK ** -0.5,
        'b_spatial': n(ks[13], (DEPTH, N_GMLP_HEADS, CHUNK)) * 0.02,
        'w_out': n(ks[14], (DEPTH, MIX_WIDTH, D_MODEL)) * MIX_WIDTH ** -0.5,
        'w_mlp1': n(ks[15], (DEPTH, D_MODEL, D_FF)) * D_MODEL ** -0.5,
        'w_mlp2': n(ks[16], (DEPTH, D_FF, D_MODEL)) * D_FF ** -0.5,
    }


def reference(x, c, ctx, c_ctx, w_ada, b_ada, norm1_g, norm2_g, w_in, q_norm_g, k_norm_g, gmlp_v_g,
              w_spatial, b_spatial, w_out, w_mlp1, w_mlp2):
    for l in range(DEPTH):
        x, ctx = layer(x, ctx, c, c_ctx, w_ada[l], b_ada[l], norm1_g[l], norm2_g[l], w_in[l],
                       q_norm_g[l], k_norm_g[l], gmlp_v_g[l], w_spatial[l], b_spatial[l], w_out[l],
                       w_mlp1[l], w_mlp2[l], l < DEPTH - 1)
    return x
```

```python
import functools

import numpy as np
import jax
import jax.numpy as jnp
from jax import lax
from jax.experimental import pallas as pl
from jax.experimental.pallas import tpu as pltpu

GRID_W = 64
HEAD_DIM = 64
AXIS_DIM = HEAD_DIM // 2
ROT_HALF = AXIS_DIM // 2
GQA_GROUP = 4
FOURIER_GROUP = 64
GMLP_HEAD = 64
CHUNK = 128
ROPE_THETA = 10000.0
EPS = 1e-6

VMEM_LIMIT_BYTES = 56 * 1024 * 1024
KEY_CHUNK = 256
DFT_RADIX = 64

BF16 = jnp.bfloat16
F32 = jnp.float32


def _params(*sem):
    return pltpu.CompilerParams(dimension_semantics=sem, vmem_limit_bytes=VMEM_LIMIT_BYTES)


def _dot(a, b):
    return jnp.dot(a, b, preferred_element_type=F32)


def _dot_nt(a, b):
    return lax.dot_general(a, b, (((1,), (1,)), ((), ())), preferred_element_type=F32)


def _rms_rows(x):
    return x * lax.rsqrt(jnp.mean(x * x, axis=-1, keepdims=True) + EPS)


def _adaln_kernel(c_ref, w_ref, b_ref, o_ref):
    h = jax.nn.silu(c_ref[...]).astype(BF16)
    o_ref[...] = _dot(h, w_ref[...].astype(BF16)) + b_ref[...]


def _adaln(cond, w, b):
    rows, d = cond.shape
    n = w.shape[1]
    tn = 1536
    return pl.pallas_call(
        _adaln_kernel,
        grid=(n // tn,),
        in_specs=[pl.BlockSpec((rows, d), lambda j: (0, 0)),
                  pl.BlockSpec((d, tn), lambda j: (0, j)),
                  pl.BlockSpec((1, tn), lambda j: (0, j))],
        out_specs=pl.BlockSpec((rows, tn), lambda j: (0, j)),
        out_shape=jax.ShapeDtypeStruct((rows, n), F32),
        compiler_params=_params("arbitrary"),
        name="adaln",
    )(cond, w, b.reshape(1, n))


def _head_norm_rope_t(p, gain, ct, st):
    r = lax.rsqrt(jnp.mean(p * p, axis=0, keepdims=True) + EPS)
    y = p * r * gain
    h = ROT_HALF
    ysw = jnp.concatenate([y[h:2 * h], y[0:h], y[3 * h:4 * h], y[2 * h:3 * h]], axis=0)
    return y * ct + ysw * st


def _in_proj_kernel(x_ref, sh_ref, sc_ref, g1_ref, wt_ref, wn_ref, qg_ref, kg_ref, ct_ref, st_ref,
                    cs_ref, vg_ref, ws_ref, bs_ref,
                    q_ref, k_ref, v_ref, g_ref, gm_ref, *, n_q_heads, n_kv_heads):
    tm = x_ref.shape[1]
    x = x_ref[0]
    h = (_rms_rows(x) * g1_ref[...]) * (1.0 + sc_ref[0]) + sh_ref[0]
    h = h.astype(BF16)

    pt = _dot_nt(wt_ref[...], h)
    ct = ct_ref[...]
    st = st_ref[...]
    qg = qg_ref[...]
    kg = kg_ref[...]
    d = HEAD_DIM
    for i in range(n_q_heads):
        qh = _head_norm_rope_t(pt[i * d:(i + 1) * d], qg, ct, st) * (d ** -0.5)
        q_ref[0, i * d:(i + 1) * d, :] = qh.astype(BF16)
    ko = n_q_heads * d
    kt = jnp.concatenate([_head_norm_rope_t(pt[ko + i * d:ko + (i + 1) * d], kg, ct, st)
                          for i in range(n_kv_heads)], axis=0)
    k_ref[0] = kt.T.astype(BF16)
    vo = ko + n_kv_heads * d
    vt = pt[vo:vo + n_kv_heads * d].astype(BF16)
    for j in range(tm // KEY_CHUNK):
        v_ref[0, j] = vt[:, j * KEY_CHUNK:(j + 1) * KEY_CHUNK]

    pn = _dot(h, wn_ref[...])
    fw = cs_ref.shape[0]
    fcs = _dot(pn[:, :fw].astype(BF16), cs_ref[...])
    g_ref[0, 0] = fcs[:, :fw].astype(BF16)
    g_ref[0, 1] = fcs[:, fw:].astype(BF16)

    gw = vg_ref.shape[1]
    n_heads = gw // GMLP_HEAD
    u = jax.nn.gelu(pn[:, fw:fw + gw])
    gv = jax.nn.gelu(pn[:, fw + gw:fw + 2 * gw])
    head_of_lane = lax.broadcasted_iota(jnp.int32, (1, gw), 1) // GMLP_HEAD
    sq = gv * gv
    rinv = jnp.zeros_like(gv)
    for i in range(n_heads):
        m = head_of_lane == i
        ms = jnp.sum(jnp.where(m, sq, 0.0), axis=-1, keepdims=True) * (1.0 / GMLP_HEAD)
        rinv = jnp.where(m, lax.rsqrt(ms + EPS), rinv)
    vn = (gv * rinv * vg_ref[...]).astype(BF16)
    bs = bs_ref[...]
    for c in range(tm // CHUNK):
        vc = vn[c * CHUNK:(c + 1) * CHUNK]
        sp = bs
        for i in range(n_heads):
            sp = sp + jnp.where(head_of_lane == i, _dot(ws_ref[i], vc), 0.0)
        gm_ref[0, c * CHUNK:(c + 1) * CHUNK, :] = (u[c * CHUNK:(c + 1) * CHUNK] * sp).astype(BF16)


def _in_proj(x, shift, scale, g1, wt, wn, qg, kg, ct, st, cs, vg, ws, bs, *, tm):
    b, n, d = x.shape
    n_kv = 2
    n_q = n_kv * GQA_GROUP
    qw = n_q * HEAD_DIM
    kw = n_kv * HEAD_DIM
    fw = cs.shape[0]
    gw = vg.shape[1]
    cpt = tm // KEY_CHUNK
    const2 = lambda bi, i: (0, 0)
    const3 = lambda bi, i: (0, 0, 0)
    kern = functools.partial(_in_proj_kernel, n_q_heads=n_q, n_kv_heads=n_kv)
    return pl.pallas_call(
        kern,
        grid=(b, n // tm),
        in_specs=[pl.BlockSpec((1, tm, d), lambda bi, i: (bi, i, 0)),
                  pl.BlockSpec((1, 1, d), lambda bi, i: (bi, 0, 0)),
                  pl.BlockSpec((1, 1, d), lambda bi, i: (bi, 0, 0)),
                  pl.BlockSpec((1, d), const2),
                  pl.BlockSpec(wt.shape, const2),
                  pl.BlockSpec(wn.shape, const2),
                  pl.BlockSpec((HEAD_DIM, 1), const2),
                  pl.BlockSpec((HEAD_DIM, 1), const2),
                  pl.BlockSpec((HEAD_DIM, tm), lambda bi, i: (0, i)),
                  pl.BlockSpec((HEAD_DIM, tm), lambda bi, i: (0, i)),
                  pl.BlockSpec(cs.shape, const2),
                  pl.BlockSpec((1, gw), const2),
                  pl.BlockSpec(ws.shape, const3),
                  pl.BlockSpec(bs.shape, const2)],
        out_specs=[pl.BlockSpec((1, qw, tm), lambda bi, i: (bi, 0, i)),
                   pl.BlockSpec((1, tm, kw), lambda bi, i: (bi, i, 0)),
                   pl.BlockSpec((1, cpt, kw, KEY_CHUNK), lambda bi, i: (bi, i, 0, 0)),
                   pl.BlockSpec((1, 2, tm, fw), lambda bi, i: (bi, 0, i, 0)),
                   pl.BlockSpec((1, tm, gw), lambda bi, i: (bi, i, 0))],
        out_shape=[jax.ShapeDtypeStruct((b, qw, n), BF16),
                   jax.ShapeDtypeStruct((b, n, kw), BF16),
                   jax.ShapeDtypeStruct((b, n // KEY_CHUNK, kw, KEY_CHUNK), BF16),
                   jax.ShapeDtypeStruct((b, 2, n, fw), BF16),
                   jax.ShapeDtypeStruct((b, n, gw), BF16)],
        compiler_params=_params("arbitrary", "arbitrary"),
        name="in_proj",
    )(x, shift, scale, g1, wt, wn, qg, kg, ct, st, cs, vg, ws, bs)


def _attn_kernel(q_ref, k_ref, v_ref, o_ref, *, n_kv_heads, n_chunks):
    tq = q_ref.shape[2]
    d = HEAD_DIM
    ones = jnp.ones((16, KEY_CHUNK), BF16)
    zpad = jnp.zeros((d, tq), BF16)
    for kv in range(n_kv_heads):
        outs = []
        for g in range(GQA_GROUP):
            hq = kv * GQA_GROUP + g
            qh = q_ref[0, hq * d:(hq + 1) * d, :]
            parts = [zpad] * n_kv_heads
            parts[kv] = qh
            qpad = jnp.concatenate(parts, axis=0)

            def step(c, carry):
                m_old, acc = carry
                start = pl.multiple_of(c * KEY_CHUNK, KEY_CHUNK)
                kc = k_ref[0, pl.ds(start, KEY_CHUNK), :]
                s = _dot(kc, qpad)
                m_new = jnp.maximum(m_old, jnp.max(s, axis=0, keepdims=True))
                alpha = jnp.exp(m_old - m_new)
                p = jnp.exp((s - m_new).astype(BF16))
                va = jnp.concatenate([v_ref[0, c, kv * d:(kv + 1) * d, :], ones], axis=0)
                return m_new, acc * alpha + _dot(va, p)

            m0 = jnp.full((1, tq), -1e30, F32)
            acc0 = jnp.zeros((d + 16, tq), F32)
            _, acc = lax.fori_loop(0, n_chunks, step, (m0, acc0))
            outs.append(acc[:d] * (1.0 / acc[d:d + 1]))
        for j in range(GQA_GROUP // 2):
            pair = jnp.concatenate(outs[2 * j:2 * j + 2], axis=0)
            col = (kv * GQA_GROUP + 2 * j) * d
            o_ref[0, :, col:col + 2 * d] = pair.T.astype(BF16)


def _attention(qt, k, vt, *, tq):
    b, qw, s = qt.shape
    t, kw = k.shape[1], k.shape[2]
    n_chunks = t // KEY_CHUNK
    kern = functools.partial(_attn_kernel, n_kv_heads=kw // HEAD_DIM, n_chunks=n_chunks)
    return pl.pallas_call(
        kern,
        grid=(b, s // tq),
        in_specs=[pl.BlockSpec((1, qw, tq), lambda bi, i: (bi, 0, i)),
                  pl.BlockSpec((1, t, kw), lambda bi, i: (bi, 0, 0)),
                  pl.BlockSpec((1, n_chunks, kw, KEY_CHUNK), lambda bi, i: (bi, 0, 0, 0))],
        out_specs=pl.BlockSpec((1, tq, qw), lambda bi, i: (bi, i, 0)),
        out_shape=jax.ShapeDtypeStruct((b, s, qw), BF16),
        compiler_params=_params("arbitrary", "arbitrary"),
        name="attention",
    )(qt, k, vt)


def _fourier_kernel(chi_ref, shi_ref, clo_ref, slo_ref, g_ref, o_ref, w_scr, *, scale):
    n = clo_ref.shape[1]

    @pl.when(pl.program_id(1) == 0)
    def _():
        clo = clo_ref[...]
        slo = slo_ref[...]
        for j in range(chi_ref.shape[0]):
            chi = chi_ref[j:j + 1, :]
            shi = shi_ref[j:j + 1, :]
            rows = slice(j * DFT_RADIX, (j + 1) * DFT_RADIX)
            w_scr[rows, 0:n] = (chi * clo - shi * slo).astype(BF16)
            w_scr[rows, n:2 * n] = (-(shi * clo + chi * slo)).astype(BF16)

    o_ref[0] = (_dot(w_scr[...], g_ref[0]) * scale).astype(BF16)


def _dft_tables(n):
    hi = n // DFT_RADIX
    t = np.arange(n, dtype=np.int64)
    a_hi = (DFT_RADIX * np.arange(hi, dtype=np.int64)[:, None] * t[None, :]) % n
    a_lo = (np.arange(DFT_RADIX, dtype=np.int64)[:, None] * t[None, :]) % n
    f = lambda fn, a: jnp.asarray(fn(2.0 * np.pi * a / n), dtype=F32)
    return f(np.cos, a_hi), f(np.sin, a_hi), f(np.cos, a_lo), f(np.sin, a_lo)


def _fourier(g):
    b, n2, fw = g.shape
    n = n2 // 2
    chi, shi, clo, slo = _dft_tables(n)
    hi_per_tile = min(8, n // DFT_RADIX)
    tmf = hi_per_tile * DFT_RADIX
    kern = functools.partial(_fourier_kernel, scale=float(1.0 / np.sqrt(n * FOURIER_GROUP)))
    return pl.pallas_call(
        kern,
        grid=(n // tmf, b),
        in_specs=[pl.BlockSpec((hi_per_tile, n), lambda i, bi: (i, 0)),
                  pl.BlockSpec((hi_per_tile, n), lambda i, bi: (i, 0)),
                  pl.BlockSpec((DFT_RADIX, n), lambda i, bi: (0, 0)),
                  pl.BlockSpec((DFT_RADIX, n), lambda i, bi: (0, 0)),
                  pl.BlockSpec((1, n2, fw), lambda i, bi: (bi, 0, 0))],
        out_specs=pl.BlockSpec((1, tmf, fw), lambda i, bi: (bi, i, 0)),
        out_shape=jax.ShapeDtypeStruct((b, n, fw), BF16),
        scratch_shapes=[pltpu.VMEM((tmf, n2), BF16)],
        compiler_params=_params("arbitrary", "arbitrary"),
        name="fourier",
    )(chi, shi, clo, slo, g)


def _out_mlp_kernel(x_ref, a_ref, f_ref, m_ref, wo_ref, g2_ref, gt1_ref, sh_ref, sc_ref, gt2_ref,
                    w1_ref, w2_ref, o_ref, *, ff_chunk):
    aw = a_ref.shape[2]
    fw = f_ref.shape[2]
    mix = (_dot(a_ref[0], wo_ref[0:aw]) + _dot(f_ref[0], wo_ref[aw:aw + fw])
           + _dot(m_ref[0], wo_ref[aw + fw:]))
    x1 = x_ref[0] + gt1_ref[0] * mix
    h = ((_rms_rows(x1) * g2_ref[...]) * (1.0 + sc_ref[0]) + sh_ref[0]).astype(BF16)
    y = jnp.zeros_like(x1)
    for j in range(w1_ref.shape[1] // ff_chunk):
        cols = slice(j * ff_chunk, (j + 1) * ff_chunk)
        a = jnp.square(jnp.maximum(_dot(h, w1_ref[:, cols]), 0.0)).astype(BF16)
        y = y + _dot(a, w2_ref[cols, :])
    o_ref[0] = x1 + gt2_ref[0] * y


def _out_mlp(x, attn, four, gm, wo, g2, gt1, shift, scale, gt2, w1, w2, *, tm):
    b, n, d = x.shape
    tok = lambda w: pl.BlockSpec((1, tm, w), lambda bi, i: (bi, i, 0))
    per_b = pl.BlockSpec((1, 1, d), lambda bi, i: (bi, 0, 0))
    const = lambda a: pl.BlockSpec(a.shape, lambda bi, i: (0, 0), pipeline_mode=pl.Buffered(1))
    kern = functools.partial(_out_mlp_kernel, ff_chunk=1024)
    return pl.pallas_call(
        kern,
        grid=(b, n // tm),
        in_specs=[tok(d), tok(attn.shape[2]), tok(four.shape[2]), tok(gm.shape[2]),
                  const(wo), pl.BlockSpec((1, d), lambda bi, i: (0, 0)),
                  per_b, per_b, per_b, per_b, const(w1), const(w2)],
        out_specs=tok(d),
        out_shape=jax.ShapeDtypeStruct((b, n, d), F32),
        compiler_params=_params("arbitrary", "arbitrary"),
        name="out_mlp",
    )(x, attn, four, gm, wo, g2, gt1, shift, scale, gt2, w1, w2)


def _rope_tables_t(n):
    rows = n // GRID_W
    row = jnp.repeat(jnp.arange(rows, dtype=F32), GRID_W)
    col = jnp.tile(jnp.arange(GRID_W, dtype=F32), rows)
    inv = ROPE_THETA ** (-jnp.arange(0, AXIS_DIM, 2, dtype=F32) / AXIS_DIM)
    ar = (row[:, None] * inv).T
    ac = (col[:, None] * inv).T
    ct = jnp.concatenate([jnp.cos(ar), jnp.cos(ar), jnp.cos(ac), jnp.cos(ac)], axis=0)
    st = jnp.concatenate([-jnp.sin(ar), jnp.sin(ar), -jnp.sin(ac), jnp.sin(ac)], axis=0)
    return ct, st


def _channel_dft():
    c = np.arange(FOURIER_GROUP)
    ang = 2.0 * np.pi * ((c[:, None] * c[None, :]) % FOURIER_GROUP) / FOURIER_GROUP
    return np.cos(ang), np.sin(ang)


def kernel(x, c, ctx, c_ctx, w_ada, b_ada, norm1_g, norm2_g, w_in, q_norm_g, k_norm_g, gmlp_v_g,
           w_spatial, b_spatial, w_out, w_mlp1, w_mlp2):
    depth = w_ada.shape[0]
    b, s, d = x.shape
    n_ctx = ctx.shape[1]
    gw = gmlp_v_g.shape[1]
    n_gmlp_heads = gw // GMLP_HEAD
    fw = d // 4
    qkv_w = w_in.shape[2] - fw - 2 * gw

    cond = jnp.zeros((16, d), F32).at[:b].set(c).at[b].set(c_ctx)
    ct_x, st_x = _rope_tables_t(s)
    ct_c = jnp.ones((HEAD_DIM, n_ctx), F32)
    st_c = jnp.zeros((HEAD_DIM, n_ctx), F32)
    cc, sc = _channel_dft()
    eye = np.eye(fw // FOURIER_GROUP)
    cs = jnp.asarray(np.concatenate([np.kron(eye, cc), np.kron(eye, sc)], axis=1), dtype=F32).astype(BF16)

    tm_x = 512 if s % 512 == 0 else KEY_CHUNK
    tm_c = KEY_CHUNK

    for l in range(depth):
        mod = _adaln(cond, w_ada[l], b_ada[l])
        mx = [m[:, None, :] for m in jnp.split(mod[:b], 6, axis=-1)]
        mc = [jnp.broadcast_to(m[None, :, :], (b, 1, d)) for m in jnp.split(mod[b:b + 1], 6, axis=-1)]

        wt = w_in[l][:, :qkv_w].T.astype(BF16)
        wn = w_in[l][:, qkv_w:].astype(BF16)
        wts = dict(g1=norm1_g[l].reshape(1, d), wt=wt, wn=wn,
                   qg=q_norm_g[l].reshape(HEAD_DIM, 1), kg=k_norm_g[l].reshape(HEAD_DIM, 1),
                   cs=cs, vg=gmlp_v_g[l].reshape(1, gw), ws=w_spatial[l].astype(BF16),
                   bs=jnp.repeat(b_spatial[l].T, GMLP_HEAD, axis=1))
        q_x, k_x, v_x, g_x, gm_x = _in_proj(x, mx[0], mx[1], ct=ct_x, st=st_x, tm=tm_x, **wts)
        q_c, k_c, v_c, g_c, gm_c = _in_proj(ctx, mc[0], mc[1], ct=ct_c, st=st_c, tm=tm_c, **wts)

        k_all = jnp.concatenate([k_c, k_x], axis=1)
        v_all = jnp.concatenate([v_c, v_x], axis=1)
        a_x = _attention(q_x, k_all, v_all, tq=KEY_CHUNK)
        f_x = _fourier(g_x.reshape(b, 2 * s, fw))

        wo = w_out[l].astype(BF16)
        w1 = w_mlp1[l].astype(BF16)
        w2 = w_mlp2[l].astype(BF16)
        g2 = norm2_g[l].reshape(1, d)
        x = _out_mlp(x, a_x, f_x, gm_x, wo, g2, mx[2], mx[3], mx[4], mx[5], w1, w2, tm=tm_x)

        if l < depth - 1:
            a_c = _attention(q_c, k_c, v_c, tq=KEY_CHUNK)
            f_c = _fourier(g_c.reshape(b, 2 * n_ctx, fw))
            ctx = _out_mlp(ctx, a_c, f_c, gm_c, wo, g2, mc[2], mc[3], mc[4], mc[5], w1, w2, tm=tm_c)
    return x
```

```python
import functools

import numpy as np
import jax
import jax.numpy as jnp
from jax import lax
from jax.experimental import pallas as pl
from jax.experimental.pallas import tpu as pltpu

GRID_W = 64
HEAD_DIM = 64
AXIS_DIM = HEAD_DIM // 2
ROT_HALF = AXIS_DIM // 2
GQA_GROUP = 4
FOURIER_GROUP = 64
GMLP_HEAD = 64
CHUNK = 128
ROPE_THETA = 10000.0
EPS = 1e-6

VMEM_LIMIT_BYTES = 56 * 1024 * 1024
KEY_CHUNK = 256
DFT_RADIX = 64

BF16 = jnp.bfloat16
F32 = jnp.float32


def _params(*sem):
    return pltpu.CompilerParams(dimension_semantics=sem, vmem_limit_bytes=VMEM_LIMIT_BYTES)


def _dot(a, b):
    return jnp.dot(a, b, preferred_element_type=F32)


def _dot_nt(a, b):
    return lax.dot_general(a, b, (((1,), (1,)), ((), ())), preferred_element_type=F32)


def _rms_rows(x):
    return x * lax.rsqrt(jnp.mean(x * x, axis=-1, keepdims=True) + EPS)


def _adaln_kernel(c_ref, w_ref, b_ref, o_ref):
    h = jax.nn.silu(c_ref[...]).astype(BF16)
    o_ref[...] = _dot(h, w_ref[...].astype(BF16)) + b_ref[...]


def _adaln(cond, w, b):
    rows, d = cond.shape
    n = w.shape[1]
    tn = 1536
    return pl.pallas_call(
        _adaln_kernel,
        grid=(n // tn,),
        in_specs=[pl.BlockSpec((rows, d), lambda j: (0, 0)),
                  pl.BlockSpec((d, tn), lambda j: (0, j)),
                  pl.BlockSpec((1, tn), lambda j: (0, j))],
        out_specs=pl.BlockSpec((rows, tn), lambda j: (0, j)),
        out_shape=jax.ShapeDtypeStruct((rows, n), F32),
        compiler_params=_params("arbitrary"),
        name="adaln",
    )(cond, w, b.reshape(1, n))


def _head_norm_rope_t(p, gain, ct, st):
    r = lax.rsqrt(jnp.mean(p * p, axis=0, keepdims=True) + EPS)
    y = p * r * gain
    h = ROT_HALF
    ysw = jnp.concatenate([y[h:2 * h], y[0:h], y[3 * h:4 * h], y[2 * h:3 * h]], axis=0)
    return y * ct + ysw * st


def _in_proj_kernel(x_ref, sh_ref, sc_ref, g1_ref, wt_ref, wn_ref, qg_ref, kg_ref, ct_ref, st_ref,
                    cs_ref, vg_ref, ws_ref, bs_ref,
                    q_ref, k_ref, v_ref, g_ref, gm_ref, *, n_q_heads, n_kv_heads):
    tm = x_ref.shape[1]
    x = x_ref[0]
    h = (_rms_rows(x) * g1_ref[...]) * (1.0 + sc_ref[0]) + sh_ref[0]
    h = h.astype(BF16)

    pt = _dot_nt(wt_ref[...], h)
    ct = ct_ref[...]
    st = st_ref[...]
    qg = qg_ref[...]
    kg = kg_ref[...]
    d = HEAD_DIM
    for i in range(n_q_heads):
        qh = _head_norm_rope_t(pt[i * d:(i + 1) * d], qg, ct, st) * (d ** -0.5)
        q_ref[0, i * d:(i + 1) * d, :] = qh.astype(BF16)
    ko = n_q_heads * d
    for i in range(n_kv_heads):
        kh = _head_norm_rope_t(pt[ko + i * d:ko + (i + 1) * d], kg, ct, st)
        k_ref[0, i] = jnp.concatenate([kh, jnp.zeros_like(kh)], axis=0).T.astype(BF16)
    vo = ko + n_kv_heads * d
    vt = pt[vo:vo + n_kv_heads * d].astype(BF16)
    for j in range(tm // KEY_CHUNK):
        v_ref[0, j] = vt[:, j * KEY_CHUNK:(j + 1) * KEY_CHUNK]

    pn = _dot(h, wn_ref[...])
    fw = cs_ref.shape[0]
    fcs = _dot(pn[:, :fw].astype(BF16), cs_ref[...])
    g_ref[0, 0] = fcs[:, :fw].astype(BF16)
    g_ref[0, 1] = fcs[:, fw:].astype(BF16)

    gw = vg_ref.shape[1]
    n_heads = gw // GMLP_HEAD
    u = jax.nn.gelu(pn[:, fw:fw + gw])
    gv = jax.nn.gelu(pn[:, fw + gw:fw + 2 * gw])
    head_of_lane = lax.broadcasted_iota(jnp.int32, (1, gw), 1) // GMLP_HEAD
    sq = gv * gv
    rinv = jnp.zeros_like(gv)
    for i in range(n_heads):
        m = head_of_lane == i
        ms = jnp.sum(jnp.where(m, sq, 0.0), axis=-1, keepdims=True) * (1.0 / GMLP_HEAD)
        rinv = jnp.where(m, lax.rsqrt(ms + EPS), rinv)
    vn = (gv * rinv * vg_ref[...]).astype(BF16)
    bs = bs_ref[...]
    for c in range(tm // CHUNK):
        vc = vn[c * CHUNK:(c + 1) * CHUNK]
        sp = bs
        for i in range(n_heads):
            sp = sp + jnp.where(head_of_lane == i, _dot(ws_ref[i], vc), 0.0)
        gm_ref[0, c * CHUNK:(c + 1) * CHUNK, :] = (u[c * CHUNK:(c + 1) * CHUNK] * sp).astype(BF16)


def _in_proj(x, shift, scale, g1, wt, wn, qg, kg, ct, st, cs, vg, ws, bs, *, tm):
    b, n, d = x.shape
    n_kv = 2
    n_q = n_kv * GQA_GROUP
    qw = n_q * HEAD_DIM
    kw = n_kv * HEAD_DIM
    fw = cs.shape[0]
    gw = vg.shape[1]
    cpt = tm // KEY_CHUNK
    const2 = lambda bi, i: (0, 0)
    const3 = lambda bi, i: (0, 0, 0)
    kern = functools.partial(_in_proj_kernel, n_q_heads=n_q, n_kv_heads=n_kv)
    return pl.pallas_call(
        kern,
        grid=(b, n // tm),
        in_specs=[pl.BlockSpec((1, tm, d), lambda bi, i: (bi, i, 0)),
                  pl.BlockSpec((1, 1, d), lambda bi, i: (bi, 0, 0)),
                  pl.BlockSpec((1, 1, d), lambda bi, i: (bi, 0, 0)),
                  pl.BlockSpec((1, d), const2),
                  pl.BlockSpec(wt.shape, const2),
                  pl.BlockSpec(wn.shape, const2),
                  pl.BlockSpec((HEAD_DIM, 1), const2),
                  pl.BlockSpec((HEAD_DIM, 1), const2),
                  pl.BlockSpec((HEAD_DIM, tm), lambda bi, i: (0, i)),
                  pl.BlockSpec((HEAD_DIM, tm), lambda bi, i: (0, i)),
                  pl.BlockSpec(cs.shape, const2),
                  pl.BlockSpec((1, gw), const2),
                  pl.BlockSpec(ws.shape, const3),
                  pl.BlockSpec(bs.shape, const2)],
        out_specs=[pl.BlockSpec((1, qw, tm), lambda bi, i: (bi, 0, i)),
                   pl.BlockSpec((1, n_kv, tm, 2 * HEAD_DIM), lambda bi, i: (bi, 0, i, 0)),
                   pl.BlockSpec((1, cpt, kw, KEY_CHUNK), lambda bi, i: (bi, i, 0, 0)),
                   pl.BlockSpec((1, 2, tm, fw), lambda bi, i: (bi, 0, i, 0)),
                   pl.BlockSpec((1, tm, gw), lambda bi, i: (bi, i, 0))],
        out_shape=[jax.ShapeDtypeStruct((b, qw, n), BF16),
                   jax.ShapeDtypeStruct((b, n_kv, n, 2 * HEAD_DIM), BF16),
                   jax.ShapeDtypeStruct((b, n // KEY_CHUNK, kw, KEY_CHUNK), BF16),
                   jax.ShapeDtypeStruct((b, 2, n, fw), BF16),
                   jax.ShapeDtypeStruct((b, n, gw), BF16)],
        compiler_params=_params("arbitrary", "arbitrary"),
        name="in_proj",
    )(x, shift, scale, g1, wt, wn, qg, kg, ct, st, cs, vg, ws, bs)


def _attn_kernel(q_ref, k_ref, v_ref, o_ref, *, n_chunks, unroll):
    tq = q_ref.shape[2]
    d = HEAD_DIM
    qcat = jnp.concatenate([q_ref[0, g * d:(g + 1) * d, :] for g in range(GQA_GROUP)], axis=1)
    qcat = jnp.concatenate([qcat, jnp.zeros_like(qcat)], axis=0)
    ones = jnp.ones((16, KEY_CHUNK), BF16)

    def step(c, carry):
        m_old, acc = carry
        start = pl.multiple_of(c * KEY_CHUNK, KEY_CHUNK)
        s = _dot(k_ref[0, 0, pl.ds(start, KEY_CHUNK), :], qcat)
        m_new = jnp.maximum(m_old, jnp.max(s, axis=0, keepdims=True))
        alpha = jnp.exp(m_old - m_new)
        p = jnp.exp((s - m_new).astype(BF16))
        va = jnp.concatenate([v_ref[0, c], ones], axis=0)
        return m_new, acc * alpha + _dot(va, p)

    m0 = jnp.full((1, GQA_GROUP * tq), -1e30, F32)
    acc0 = jnp.zeros((d + 16, GQA_GROUP * tq), F32)
    _, acc = lax.fori_loop(0, n_chunks, step, (m0, acc0), unroll=unroll)
    out = acc[:d] * (1.0 / acc[d:d + 1])
    for j in range(GQA_GROUP // 2):
        pair = jnp.concatenate([out[:, (2 * j) * tq:(2 * j + 1) * tq],
                                out[:, (2 * j + 1) * tq:(2 * j + 2) * tq]], axis=0)
        o_ref[0, :, 2 * j * d:(2 * j + 2) * d] = pair.T.astype(BF16)


def _attention(qt, k, vt, *, tq, unroll=True):
    b, qw, s = qt.shape
    n_kv, t, kw = k.shape[1], k.shape[2], k.shape[3]
    n_chunks = t // KEY_CHUNK
    gq = GQA_GROUP * HEAD_DIM
    kern = functools.partial(_attn_kernel, n_chunks=n_chunks, unroll=unroll)
    return pl.pallas_call(
        kern,
        grid=(b, n_kv, s // tq),
        in_specs=[pl.BlockSpec((1, gq, tq), lambda bi, kv, i: (bi, kv, i)),
                  pl.BlockSpec((1, 1, t, kw), lambda bi, kv, i: (bi, kv, 0, 0)),
                  pl.BlockSpec((1, n_chunks, HEAD_DIM, KEY_CHUNK), lambda bi, kv, i: (bi, 0, kv, 0))],
        out_specs=pl.BlockSpec((1, tq, gq), lambda bi, kv, i: (bi, i, kv)),
        out_shape=jax.ShapeDtypeStruct((b, s, qw), BF16),
        compiler_params=_params("arbitrary", "arbitrary", "arbitrary"),
        name="attention",
    )(qt, k, vt)


def _fourier_kernel(chi_ref, shi_ref, clo_ref, slo_ref, g_ref, o_ref, w_scr, *, scale):
    n = clo_ref.shape[1]

    @pl.when(pl.program_id(1) == 0)
    def _():
        clo = clo_ref[...]
        slo = slo_ref[...]
        for j in range(chi_ref.shape[0]):
            chi = chi_ref[j:j + 1, :]
            shi = shi_ref[j:j + 1, :]
            rows = slice(j * DFT_RADIX, (j + 1) * DFT_RADIX)
            w_scr[rows, 0:n] = (chi * clo - shi * slo).astype(BF16)
            w_scr[rows, n:2 * n] = (-(shi * clo + chi * slo)).astype(BF16)

    o_ref[0] = (_dot(w_scr[...], g_ref[0]) * scale).astype(BF16)


def _dft_tables(n):
    hi = n // DFT_RADIX
    t = np.arange(n, dtype=np.int64)
    a_hi = (DFT_RADIX * np.arange(hi, dtype=np.int64)[:, None] * t[None, :]) % n
    a_lo = (np.arange(DFT_RADIX, dtype=np.int64)[:, None] * t[None, :]) % n
    f = lambda fn, a: jnp.asarray(fn(2.0 * np.pi * a / n), dtype=F32)
    return f(np.cos, a_hi), f(np.sin, a_hi), f(np.cos, a_lo), f(np.sin, a_lo)


def _fourier(g):
    b, n2, fw = g.shape
    n = n2 // 2
    chi, shi, clo, slo = _dft_tables(n)
    hi_per_tile = min(8, n // DFT_RADIX)
    tmf = hi_per_tile * DFT_RADIX
    kern = functools.partial(_fourier_kernel, scale=float(1.0 / np.sqrt(n * FOURIER_GROUP)))
    return pl.pallas_call(
        kern,
        grid=(n // tmf, b),
        in_specs=[pl.BlockSpec((hi_per_tile, n), lambda i, bi: (i, 0)),
                  pl.BlockSpec((hi_per_tile, n), lambda i, bi: (i, 0)),
                  pl.BlockSpec((DFT_RADIX, n), lambda i, bi: (0, 0)),
                  pl.BlockSpec((DFT_RADIX, n), lambda i, bi: (0, 0)),
                  pl.BlockSpec((1, n2, fw), lambda i, bi: (bi, 0, 0))],
        out_specs=pl.BlockSpec((1, tmf, fw), lambda i, bi: (bi, i, 0)),
        out_shape=jax.ShapeDtypeStruct((b, n, fw), BF16),
        scratch_shapes=[pltpu.VMEM((tmf, n2), BF16)],
        compiler_params=_params("arbitrary", "arbitrary"),
        name="fourier",
    )(chi, shi, clo, slo, g)


def _out_mlp_kernel(x_ref, a_ref, f_ref, m_ref, wo_ref, g2_ref, gt1_ref, sh_ref, sc_ref, gt2_ref,
                    w1_ref, w2_ref, o_ref, *, ff_chunk):
    aw = a_ref.shape[2]
    fw = f_ref.shape[2]
    mix = (_dot(a_ref[0], wo_ref[0:aw]) + _dot(f_ref[0], wo_ref[aw:aw + fw])
           + _dot(m_ref[0], wo_ref[aw + fw:]))
    x1 = x_ref[0] + gt1_ref[0] * mix
    h = ((_rms_rows(x1) * g2_ref[...]) * (1.0 + sc_ref[0]) + sh_ref[0]).astype(BF16)
    y = jnp.zeros_like(x1)
    for j in range(w1_ref.shape[1] // ff_chunk):
        cols = slice(j * ff_chunk, (j + 1) * ff_chunk)
        a = jnp.square(jnp.maximum(_dot(h, w1_ref[:, cols]), 0.0)).astype(BF16)
        y = y + _dot(a, w2_ref[cols, :])
    o_ref[0] = x1 + gt2_ref[0] * y


def _out_mlp(x, attn, four, gm, wo, g2, gt1, shift, scale, gt2, w1, w2, *, tm):
    b, n, d = x.shape
    tok = lambda w: pl.BlockSpec((1, tm, w), lambda bi, i: (bi, i, 0))
    per_b = pl.BlockSpec((1, 1, d), lambda bi, i: (bi, 0, 0))
    const = lambda a: pl.BlockSpec(a.shape, lambda bi, i: (0, 0), pipeline_mode=pl.Buffered(1))
    kern = functools.partial(_out_mlp_kernel, ff_chunk=1024)
    return pl.pallas_call(
        kern,
        grid=(b, n // tm),
        in_specs=[tok(d), tok(attn.shape[2]), tok(four.shape[2]), tok(gm.shape[2]),
                  const(wo), pl.BlockSpec((1, d), lambda bi, i: (0, 0)),
                  per_b, per_b, per_b, per_b, const(w1), const(w2)],
        out_specs=tok(d),
        out_shape=jax.ShapeDtypeStruct((b, n, d), F32),
        compiler_params=_params("arbitrary", "arbitrary"),
        name="out_mlp",
    )(x, attn, four, gm, wo, g2, gt1, shift, scale, gt2, w1, w2)


def _rope_tables_t(n):
    rows = n // GRID_W
    row = jnp.repeat(jnp.arange(rows, dtype=F32), GRID_W)
    col = jnp.tile(jnp.arange(GRID_W, dtype=F32), rows)
    inv = ROPE_THETA ** (-jnp.arange(0, AXIS_DIM, 2, dtype=F32) / AXIS_DIM)
    ar = (row[:, None] * inv).T
    ac = (col[:, None] * inv).T
    ct = jnp.concatenate([jnp.cos(ar), jnp.cos(ar), jnp.cos(ac), jnp.cos(ac)], axis=0)
    st = jnp.concatenate([-jnp.sin(ar), jnp.sin(ar), -jnp.sin(ac), jnp.sin(ac)], axis=0)
    return ct, st


def _channel_dft():
    c = np.arange(FOURIER_GROUP)
    ang = 2.0 * np.pi * ((c[:, None] * c[None, :]) % FOURIER_GROUP) / FOURIER_GROUP
    return np.cos(ang), np.sin(ang)


def kernel(x, c, ctx, c_ctx, w_ada, b_ada, norm1_g, norm2_g, w_in, q_norm_g, k_norm_g, gmlp_v_g,
           w_spatial, b_spatial, w_out, w_mlp1, w_mlp2):
    depth = w_ada.shape[0]
    b, s, d = x.shape
    n_ctx = ctx.shape[1]
    gw = gmlp_v_g.shape[1]
    n_gmlp_heads = gw // GMLP_HEAD
    fw = d // 4
    qkv_w = w_in.shape[2] - fw - 2 * gw

    cond = jnp.zeros((16, d), F32).at[:b].set(c).at[b].set(c_ctx)
    ct_x, st_x = _rope_tables_t(s)
    ct_c = jnp.ones((HEAD_DIM, n_ctx), F32)
    st_c = jnp.zeros((HEAD_DIM, n_ctx), F32)
    cc, sc = _channel_dft()
    eye = np.eye(fw // FOURIER_GROUP)
    cs = jnp.asarray(np.concatenate([np.kron(eye, cc), np.kron(eye, sc)], axis=1), dtype=F32).astype(BF16)

    tm_x = 512 if s % 512 == 0 else KEY_CHUNK
    tm_c = KEY_CHUNK

    for l in range(depth):
        mod = _adaln(cond, w_ada[l], b_ada[l])
        mx = [m[:, None, :] for m in jnp.split(mod[:b], 6, axis=-1)]
        mc = [jnp.broadcast_to(m[None, :, :], (b, 1, d)) for m in jnp.split(mod[b:b + 1], 6, axis=-1)]

        wt = w_in[l][:, :qkv_w].T.astype(BF16)
        wn = w_in[l][:, qkv_w:].astype(BF16)
        wts = dict(g1=norm1_g[l].reshape(1, d), wt=wt, wn=wn,
                   qg=q_norm_g[l].reshape(HEAD_DIM, 1), kg=k_norm_g[l].reshape(HEAD_DIM, 1),
                   cs=cs, vg=gmlp_v_g[l].reshape(1, gw), ws=w_spatial[l].astype(BF16),
                   bs=jnp.repeat(b_spatial[l].T, GMLP_HEAD, axis=1))
        q_x, k_x, v_x, g_x, gm_x = _in_proj(x, mx[0], mx[1], ct=ct_x, st=st_x, tm=tm_x, **wts)
        q_c, k_c, v_c, g_c, gm_c = _in_proj(ctx, mc[0], mc[1], ct=ct_c, st=st_c, tm=tm_c, **wts)

        k_all = jnp.concatenate([k_c, k_x], axis=2)
        v_all = jnp.concatenate([v_c, v_x], axis=1)
        a_x = _attention(q_x, k_all, v_all, tq=KEY_CHUNK)
        f_x = _fourier(g_x.reshape(b, 2 * s, fw))

        wo = w_out[l].astype(BF16)
        w1 = w_mlp1[l].astype(BF16)
        w2 = w_mlp2[l].astype(BF16)
        g2 = norm2_g[l].reshape(1, d)
        x = _out_mlp(x, a_x, f_x, gm_x, wo, g2, mx[2], mx[3], mx[4], mx[5], w1, w2, tm=tm_x)

        if l < depth - 1:
            a_c = _attention(q_c, k_c, v_c, tq=KEY_CHUNK)
            f_c = _fourier(g_c.reshape(b, 2 * n_ctx, fw))
            ctx = _out_mlp(ctx, a_c, f_c, gm_c, wo, g2, mc[2], mc[3], mc[4], mc[5], w1, w2, tm=tm_c)
    return x
```

```python
import functools

import numpy as np
import jax
import jax.numpy as jnp
from jax import lax
from jax.experimental import pallas as pl
from jax.experimental.pallas import tpu as pltpu

GRID_W = 64
HEAD_DIM = 64
AXIS_DIM = HEAD_DIM // 2
ROT_HALF = AXIS_DIM // 2
GQA_GROUP = 4
FOURIER_GROUP = 64
GMLP_HEAD = 64
CHUNK = 128
ROPE_THETA = 10000.0
EPS = 1e-6

VMEM_LIMIT_BYTES = 56 * 1024 * 1024
KEY_CHUNK = 256
DFT_RADIX = 64
LOG2_E = 1.4426950408889634
Q_SCALE = HEAD_DIM ** -0.5 * LOG2_E
MAX_UNSHIFTED_LOGIT = 60.0

BF16 = jnp.bfloat16
F32 = jnp.float32


def _params(*sem):
    return pltpu.CompilerParams(dimension_semantics=sem, vmem_limit_bytes=VMEM_LIMIT_BYTES)


def _dot(a, b):
    return jnp.dot(a, b, preferred_element_type=F32)


def _dot_nt(a, b):
    return lax.dot_general(a, b, (((1,), (1,)), ((), ())), preferred_element_type=F32)


def _rms_rows(x):
    return x * lax.rsqrt(jnp.mean(x * x, axis=-1, keepdims=True) + EPS)


def _adaln_kernel(c_ref, w_ref, b_ref, o_ref):
    h = jax.nn.silu(c_ref[...]).astype(BF16)
    o_ref[...] = _dot(h, w_ref[...].astype(BF16)) + b_ref[...]


def _adaln(cond, w, b):
    rows, d = cond.shape
    n = w.shape[1]
    tn = 1536
    return pl.pallas_call(
        _adaln_kernel,
        grid=(n // tn,),
        in_specs=[pl.BlockSpec((rows, d), lambda j: (0, 0)),
                  pl.BlockSpec((d, tn), lambda j: (0, j)),
                  pl.BlockSpec((1, tn), lambda j: (0, j))],
        out_specs=pl.BlockSpec((rows, tn), lambda j: (0, j)),
        out_shape=jax.ShapeDtypeStruct((rows, n), F32),
        compiler_params=_params("arbitrary"),
        name="adaln",
    )(cond, w, b.reshape(1, n))


def _head_norm_rope_t(p, gain, ct, st):
    r = lax.rsqrt(jnp.mean(p * p, axis=0, keepdims=True) + EPS)
    y = p * r * gain
    h = ROT_HALF
    ysw = jnp.concatenate([y[h:2 * h], y[0:h], y[3 * h:4 * h], y[2 * h:3 * h]], axis=0)
    return y * ct + ysw * st


def _in_proj_kernel(x_ref, sh_ref, sc_ref, g1_ref, wt_ref, wn_ref, qg_ref, kg_ref, ct_ref, st_ref,
                    cs_ref, vg_ref, ws_ref, bs_ref,
                    q_ref, k_ref, v_ref, g_ref, gm_ref, *, n_q_heads, n_kv_heads):
    tm = x_ref.shape[1]
    x = x_ref[0]
    h = (_rms_rows(x) * g1_ref[...]) * (1.0 + sc_ref[0]) + sh_ref[0]
    h = h.astype(BF16)

    pt = _dot_nt(wt_ref[...], h)
    ct = ct_ref[...]
    st = st_ref[...]
    qg = qg_ref[...]
    kg = kg_ref[...]
    d = HEAD_DIM
    for i in range(n_q_heads):
        qh = _head_norm_rope_t(pt[i * d:(i + 1) * d], qg, ct, st) * Q_SCALE
        q_ref[0, i * d:(i + 1) * d, :] = qh.astype(BF16)
    ko = n_q_heads * d
    for i in range(n_kv_heads):
        kh = _head_norm_rope_t(pt[ko + i * d:ko + (i + 1) * d], kg, ct, st)
        k_ref[0, i] = jnp.concatenate([kh, jnp.zeros_like(kh)], axis=0).T.astype(BF16)
    vo = ko + n_kv_heads * d
    vt = pt[vo:vo + n_kv_heads * d].astype(BF16)
    for j in range(tm // KEY_CHUNK):
        v_ref[0, j] = vt[:, j * KEY_CHUNK:(j + 1) * KEY_CHUNK]

    pn = _dot(h, wn_ref[...])
    fw = cs_ref.shape[0]
    fcs = _dot(pn[:, :fw].astype(BF16), cs_ref[...])
    g_ref[0, 0] = fcs[:, :fw].astype(BF16)
    g_ref[0, 1] = fcs[:, fw:].astype(BF16)

    gw = vg_ref.shape[1]
    n_heads = gw // GMLP_HEAD
    u = jax.nn.gelu(pn[:, fw:fw + gw])
    gv = jax.nn.gelu(pn[:, fw + gw:fw + 2 * gw])
    head_of_lane = lax.broadcasted_iota(jnp.int32, (1, gw), 1) // GMLP_HEAD
    sq = gv * gv
    rinv = jnp.zeros_like(gv)
    for i in range(n_heads):
        m = head_of_lane == i
        ms = jnp.sum(jnp.where(m, sq, 0.0), axis=-1, keepdims=True) * (1.0 / GMLP_HEAD)
        rinv = jnp.where(m, lax.rsqrt(ms + EPS), rinv)
    vn = (gv * rinv * vg_ref[...]).astype(BF16)
    bs = bs_ref[...]
    for c in range(tm // CHUNK):
        vc = vn[c * CHUNK:(c + 1) * CHUNK]
        sp = bs
        for i in range(n_heads):
            sp = sp + jnp.where(head_of_lane == i, _dot(ws_ref[i], vc), 0.0)
        gm_ref[0, c * CHUNK:(c + 1) * CHUNK, :] = (u[c * CHUNK:(c + 1) * CHUNK] * sp).astype(BF16)


def _in_proj(x, shift, scale, g1, wt, wn, qg, kg, ct, st, cs, vg, ws, bs, *, tm):
    b, n, d = x.shape
    n_kv = 2
    n_q = n_kv * GQA_GROUP
    qw = n_q * HEAD_DIM
    kw = n_kv * HEAD_DIM
    fw = cs.shape[0]
    gw = vg.shape[1]
    cpt = tm // KEY_CHUNK
    const2 = lambda bi, i: (0, 0)
    const3 = lambda bi, i: (0, 0, 0)
    kern = functools.partial(_in_proj_kernel, n_q_heads=n_q, n_kv_heads=n_kv)
    return pl.pallas_call(
        kern,
        grid=(b, n // tm),
        in_specs=[pl.BlockSpec((1, tm, d), lambda bi, i: (bi, i, 0)),
                  pl.BlockSpec((1, 1, d), lambda bi, i: (bi, 0, 0)),
                  pl.BlockSpec((1, 1, d), lambda bi, i: (bi, 0, 0)),
                  pl.BlockSpec((1, d), const2),
                  pl.BlockSpec(wt.shape, const2),
                  pl.BlockSpec(wn.shape, const2),
                  pl.BlockSpec((HEAD_DIM, 1), const2),
                  pl.BlockSpec((HEAD_DIM, 1), const2),
                  pl.BlockSpec((HEAD_DIM, tm), lambda bi, i: (0, i)),
                  pl.BlockSpec((HEAD_DIM, tm), lambda bi, i: (0, i)),
                  pl.BlockSpec(cs.shape, const2),
                  pl.BlockSpec((1, gw), const2),
                  pl.BlockSpec(ws.shape, const3),
                  pl.BlockSpec(bs.shape, const2)],
        out_specs=[pl.BlockSpec((1, qw, tm), lambda bi, i: (bi, 0, i)),
                   pl.BlockSpec((1, n_kv, tm, 2 * HEAD_DIM), lambda bi, i: (bi, 0, i, 0)),
                   pl.BlockSpec((1, cpt, kw, KEY_CHUNK), lambda bi, i: (bi, i, 0, 0)),
                   pl.BlockSpec((1, 2, tm, fw), lambda bi, i: (bi, 0, i, 0)),
                   pl.BlockSpec((1, tm, gw), lambda bi, i: (bi, i, 0))],
        out_shape=[jax.ShapeDtypeStruct((b, qw, n), BF16),
                   jax.ShapeDtypeStruct((b, n_kv, n, 2 * HEAD_DIM), BF16),
                   jax.ShapeDtypeStruct((b, n // KEY_CHUNK, kw, KEY_CHUNK), BF16),
                   jax.ShapeDtypeStruct((b, 2, n, fw), BF16),
                   jax.ShapeDtypeStruct((b, n, gw), BF16)],
        compiler_params=_params("arbitrary", "arbitrary"),
        name="in_proj",
    )(x, shift, scale, g1, wt, wn, qg, kg, ct, st, cs, vg, ws, bs)


def _attn_queries(q_ref, cols):
    d = HEAD_DIM
    qcat = jnp.concatenate([q_ref[0, g * d:(g + 1) * d, cols] for g in range(GQA_GROUP)], axis=1)
    return jnp.concatenate([qcat, jnp.zeros_like(qcat)], axis=0)


def _attn_finish(acc, o_ref, rows):
    d = HEAD_DIM
    tq = acc.shape[1] // GQA_GROUP
    out = acc[:d] * (1.0 / acc[d:d + 1])
    for j in range(GQA_GROUP // 2):
        pair = jnp.concatenate([out[:, (2 * j) * tq:(2 * j + 1) * tq],
                                out[:, (2 * j + 1) * tq:(2 * j + 2) * tq]], axis=0)
        o_ref[0, rows, 2 * j * d:(2 * j + 2) * d] = pair.T.astype(BF16)


def _attn_kernel_bounded(q_ref, k_ref, v_ref, o_ref, *, n_chunks, tq):
    ones = jnp.ones((16, KEY_CHUNK), BF16)
    n_sub = q_ref.shape[2] // tq
    sub = lambda i: slice(i * tq, (i + 1) * tq)
    qcats = [_attn_queries(q_ref, sub(i)) for i in range(n_sub)]
    work = [(i, c) for i in range(n_sub) for c in range(n_chunks)]
    scores = lambda i, c: _dot(k_ref[0, 0, c * KEY_CHUNK:(c + 1) * KEY_CHUNK, :], qcats[i])
    s_next = scores(*work[0])
    acc = None
    for w, (i, c) in enumerate(work):
        s = s_next
        if w + 1 < len(work):
            s_next = scores(*work[w + 1])
        p = jnp.exp2(s).astype(BF16)
        pv = _dot(jnp.concatenate([v_ref[0, c], ones], axis=0), p)
        acc = pv if c == 0 else acc + pv
        if c == n_chunks - 1:
            _attn_finish(acc, o_ref, sub(i))


def _attn_kernel_online(q_ref, k_ref, v_ref, o_ref, *, n_chunks, tq):
    assert q_ref.shape[2] == tq
    qcat = _attn_queries(q_ref, slice(0, tq))
    ones = jnp.ones((16, KEY_CHUNK), BF16)

    def step(c, carry):
        m_old, acc = carry
        start = pl.multiple_of(c * KEY_CHUNK, KEY_CHUNK)
        s = _dot(k_ref[0, 0, pl.ds(start, KEY_CHUNK), :], qcat)
        m_new = jnp.maximum(m_old, jnp.max(s, axis=0, keepdims=True))
        alpha = jnp.exp2(m_old - m_new)
        p = jnp.exp2((s - m_new).astype(BF16))
        va = jnp.concatenate([v_ref[0, c], ones], axis=0)
        return m_new, acc * alpha + _dot(va, p)

    m0 = jnp.full((1, qcat.shape[1]), -1e30, F32)
    acc0 = jnp.zeros((HEAD_DIM + 16, qcat.shape[1]), F32)
    _, acc = lax.fori_loop(0, n_chunks, step, (m0, acc0))
    _attn_finish(acc, o_ref, slice(0, tq))


def _attention(qt, k, vt, logit_bound, *, tq):
    n_sub = 2 if qt.shape[2] % (2 * tq) == 0 else 1
    bounded = functools.partial(_attention_call, body=_attn_kernel_bounded, tq=tq, n_sub=n_sub)
    online = functools.partial(_attention_call, body=_attn_kernel_online, tq=tq, n_sub=1)
    return lax.cond(logit_bound <= MAX_UNSHIFTED_LOGIT, bounded, online, qt, k, vt)


def _attention_call(qt, k, vt, *, body, tq, n_sub):
    b, qw, s = qt.shape
    n_kv, t, kw = k.shape[1], k.shape[2], k.shape[3]
    n_chunks = t // KEY_CHUNK
    gq = GQA_GROUP * HEAD_DIM
    tb = n_sub * tq
    kern = functools.partial(body, n_chunks=n_chunks, tq=tq)
    return pl.pallas_call(
        kern,
        grid=(b, n_kv, s // tb),
        in_specs=[pl.BlockSpec((1, gq, tb), lambda bi, kv, i: (bi, kv, i)),
                  pl.BlockSpec((1, 1, t, kw), lambda bi, kv, i: (bi, kv, 0, 0)),
                  pl.BlockSpec((1, n_chunks, HEAD_DIM, KEY_CHUNK), lambda bi, kv, i: (bi, 0, kv, 0))],
        out_specs=pl.BlockSpec((1, tb, gq), lambda bi, kv, i: (bi, i, kv)),
        out_shape=jax.ShapeDtypeStruct((b, s, qw), BF16),
        compiler_params=_params("arbitrary", "arbitrary", "arbitrary"),
        name="attention",
    )(qt, k, vt)


def _fourier_kernel(chi_ref, shi_ref, clo_ref, slo_ref, g_ref, o_ref, w_scr, *, scale):
    n = clo_ref.shape[1]

    @pl.when(pl.program_id(1) == 0)
    def _():
        clo = clo_ref[...]
        slo = slo_ref[...]
        for j in range(chi_ref.shape[0]):
            chi = chi_ref[j:j + 1, :]
            shi = shi_ref[j:j + 1, :]
            rows = slice(j * DFT_RADIX, (j + 1) * DFT_RADIX)
            w_scr[rows, 0:n] = (chi * clo - shi * slo).astype(BF16)
            w_scr[rows, n:2 * n] = (-(shi * clo + chi * slo)).astype(BF16)

    o_ref[0] = (_dot(w_scr[...], g_ref[0]) * scale).astype(BF16)


def _dft_tables(n):
    hi = n // DFT_RADIX
    t = np.arange(n, dtype=np.int64)
    a_hi = (DFT_RADIX * np.arange(hi, dtype=np.int64)[:, None] * t[None, :]) % n
    a_lo = (np.arange(DFT_RADIX, dtype=np.int64)[:, None] * t[None, :]) % n
    f = lambda fn, a: jnp.asarray(fn(2.0 * np.pi * a / n), dtype=F32)
    return f(np.cos, a_hi), f(np.sin, a_hi), f(np.cos, a_lo), f(np.sin, a_lo)


def _fourier(g):
    b, n2, fw = g.shape
    n = n2 // 2
    chi, shi, clo, slo = _dft_tables(n)
    hi_per_tile = min(8, n // DFT_RADIX)
    tmf = hi_per_tile * DFT_RADIX
    kern = functools.partial(_fourier_kernel, scale=float(1.0 / np.sqrt(n * FOURIER_GROUP)))
    return pl.pallas_call(
        kern,
        grid=(n // tmf, b),
        in_specs=[pl.BlockSpec((hi_per_tile, n), lambda i, bi: (i, 0)),
                  pl.BlockSpec((hi_per_tile, n), lambda i, bi: (i, 0)),
                  pl.BlockSpec((DFT_RADIX, n), lambda i, bi: (0, 0)),
                  pl.BlockSpec((DFT_RADIX, n), lambda i, bi: (0, 0)),
                  pl.BlockSpec((1, n2, fw), lambda i, bi: (bi, 0, 0))],
        out_specs=pl.BlockSpec((1, tmf, fw), lambda i, bi: (bi, i, 0)),
        out_shape=jax.ShapeDtypeStruct((b, n, fw), BF16),
        scratch_shapes=[pltpu.VMEM((tmf, n2), BF16)],
        compiler_params=_params("arbitrary", "arbitrary"),
        name="fourier",
    )(chi, shi, clo, slo, g)


def _out_mlp_kernel(x_ref, a_ref, f_ref, m_ref, wo_ref, g2_ref, gt1_ref, sh_ref, sc_ref, gt2_ref,
                    w1_ref, w2_ref, o_ref, *, ff_chunk):
    aw = a_ref.shape[2]
    fw = f_ref.shape[2]
    mix = (_dot(a_ref[0], wo_ref[0:aw]) + _dot(f_ref[0], wo_ref[aw:aw + fw])
           + _dot(m_ref[0], wo_ref[aw + fw:]))
    x1 = x_ref[0] + gt1_ref[0] * mix
    h = ((_rms_rows(x1) * g2_ref[...]) * (1.0 + sc_ref[0]) + sh_ref[0]).astype(BF16)
    y = jnp.zeros_like(x1)
    for j in range(w1_ref.shape[1] // ff_chunk):
        cols = slice(j * ff_chunk, (j + 1) * ff_chunk)
        a = jnp.square(jnp.maximum(_dot(h, w1_ref[:, cols]), 0.0)).astype(BF16)
        y = y + _dot(a, w2_ref[cols, :])
    o_ref[0] = x1 + gt2_ref[0] * y


def _out_mlp(x, attn, four, gm, wo, g2, gt1, shift, scale, gt2, w1, w2, *, tm):
    b, n, d = x.shape
    tok = lambda w: pl.BlockSpec((1, tm, w), lambda bi, i: (bi, i, 0))
    per_b = pl.BlockSpec((1, 1, d), lambda bi, i: (bi, 0, 0))
    const = lambda a: pl.BlockSpec(a.shape, lambda bi, i: (0, 0), pipeline_mode=pl.Buffered(1))
    kern = functools.partial(_out_mlp_kernel, ff_chunk=1024)
    return pl.pallas_call(
        kern,
        grid=(b, n // tm),
        in_specs=[tok(d), tok(attn.shape[2]), tok(four.shape[2]), tok(gm.shape[2]),
                  const(wo), pl.BlockSpec((1, d), lambda bi, i: (0, 0)),
                  per_b, per_b, per_b, per_b, const(w1), const(w2)],
        out_specs=tok(d),
        out_shape=jax.ShapeDtypeStruct((b, n, d), F32),
        compiler_params=_params("arbitrary", "arbitrary"),
        name="out_mlp",
    )(x, attn, four, gm, wo, g2, gt1, shift, scale, gt2, w1, w2)


def _rope_tables_t(n):
    rows = n // GRID_W
    row = jnp.repeat(jnp.arange(rows, dtype=F32), GRID_W)
    col = jnp.tile(jnp.arange(GRID_W, dtype=F32), rows)
    inv = ROPE_THETA ** (-jnp.arange(0, AXIS_DIM, 2, dtype=F32) / AXIS_DIM)
    ar = (row[:, None] * inv).T
    ac = (col[:, None] * inv).T
    ct = jnp.concatenate([jnp.cos(ar), jnp.cos(ar), jnp.cos(ac), jnp.cos(ac)], axis=0)
    st = jnp.concatenate([-jnp.sin(ar), jnp.sin(ar), -jnp.sin(ac), jnp.sin(ac)], axis=0)
    return ct, st


def _channel_dft():
    c = np.arange(FOURIER_GROUP)
    ang = 2.0 * np.pi * ((c[:, None] * c[None, :]) % FOURIER_GROUP) / FOURIER_GROUP
    return np.cos(ang), np.sin(ang)


def kernel(x, c, ctx, c_ctx, w_ada, b_ada, norm1_g, norm2_g, w_in, q_norm_g, k_norm_g, gmlp_v_g,
           w_spatial, b_spatial, w_out, w_mlp1, w_mlp2):
    depth = w_ada.shape[0]
    b, s, d = x.shape
    n_ctx = ctx.shape[1]
    gw = gmlp_v_g.shape[1]
    n_gmlp_heads = gw // GMLP_HEAD
    fw = d // 4
    qkv_w = w_in.shape[2] - fw - 2 * gw

    cond = jnp.zeros((16, d), F32).at[:b].set(c).at[b].set(c_ctx)
    ct_x, st_x = _rope_tables_t(s)
    ct_c = jnp.ones((HEAD_DIM, n_ctx), F32)
    st_c = jnp.zeros((HEAD_DIM, n_ctx), F32)
    cc, sc = _channel_dft()
    eye = np.eye(fw // FOURIER_GROUP)
    cs = jnp.asarray(np.concatenate([np.kron(eye, cc), np.kron(eye, sc)], axis=1), dtype=F32).astype(BF16)

    tm_x = 512 if s % 512 == 0 else KEY_CHUNK
    tm_c = KEY_CHUNK

    for l in range(depth):
        mod = _adaln(cond, w_ada[l], b_ada[l])
        mx = [m[:, None, :] for m in jnp.split(mod[:b], 6, axis=-1)]
        mc = [jnp.broadcast_to(m[None, :, :], (b, 1, d)) for m in jnp.split(mod[b:b + 1], 6, axis=-1)]

        wt = w_in[l][:, :qkv_w].T.astype(BF16)
        wn = w_in[l][:, qkv_w:].astype(BF16)
        wts = dict(g1=norm1_g[l].reshape(1, d), wt=wt, wn=wn,
                   qg=q_norm_g[l].reshape(HEAD_DIM, 1), kg=k_norm_g[l].reshape(HEAD_DIM, 1),
                   cs=cs, vg=gmlp_v_g[l].reshape(1, gw), ws=w_spatial[l].astype(BF16),
                   bs=jnp.repeat(b_spatial[l].T, GMLP_HEAD, axis=1))
        q_x, k_x, v_x, g_x, gm_x = _in_proj(x, mx[0], mx[1], ct=ct_x, st=st_x, tm=tm_x, **wts)
        q_c, k_c, v_c, g_c, gm_c = _in_proj(ctx, mc[0], mc[1], ct=ct_c, st=st_c, tm=tm_c, **wts)

        logit_bound = (HEAD_DIM * Q_SCALE) * jnp.max(jnp.abs(q_norm_g[l])) * jnp.max(jnp.abs(k_norm_g[l]))
        k_all = jnp.concatenate([k_c, k_x], axis=2)
        v_all = jnp.concatenate([v_c, v_x], axis=1)
        a_x = _attention(q_x, k_all, v_all, logit_bound, tq=KEY_CHUNK)
        f_x = _fourier(g_x.reshape(b, 2 * s, fw))

        wo = w_out[l].astype(BF16)
        w1 = w_mlp1[l].astype(BF16)
        w2 = w_mlp2[l].astype(BF16)
        g2 = norm2_g[l].reshape(1, d)
        x = _out_mlp(x, a_x, f_x, gm_x, wo, g2, mx[2], mx[3], mx[4], mx[5], w1, w2, tm=tm_x)

        if l < depth - 1:
            a_c = _attention(q_c, k_c, v_c, logit_bound, tq=KEY_CHUNK)
            f_c = _fourier(g_c.reshape(b, 2 * n_ctx, fw))
            ctx = _out_mlp(ctx, a_c, f_c, gm_c, wo, g2, mc[2], mc[3], mc[4], mc[5], w1, w2, tm=tm_c)
    return x
```

```python
import functools

import numpy as np
import jax
import jax.numpy as jnp
from jax import lax
from jax.experimental import pallas as pl
from jax.experimental.pallas import tpu as pltpu

GRID_W = 64
HEAD_DIM = 64
AXIS_DIM = HEAD_DIM // 2
ROT_HALF = AXIS_DIM // 2
GQA_GROUP = 4
FOURIER_GROUP = 64
GMLP_HEAD = 64
CHUNK = 128
ROPE_THETA = 10000.0
EPS = 1e-6

VMEM_LIMIT_BYTES = 56 * 1024 * 1024
KEY_CHUNK = 256
DFT_RADIX = 64
LOG2_E = 1.4426950408889634
Q_SCALE = HEAD_DIM ** -0.5 * LOG2_E
MAX_UNSHIFTED_LOGIT = 60.0

BF16 = jnp.bfloat16
F32 = jnp.float32


def _params(*sem):
    return pltpu.CompilerParams(dimension_semantics=sem, vmem_limit_bytes=VMEM_LIMIT_BYTES)


def _dot(a, b):
    return jnp.dot(a, b, preferred_element_type=F32)


def _dot_nt(a, b):
    return lax.dot_general(a, b, (((1,), (1,)), ((), ())), preferred_element_type=F32)


def _rms_rows(x):
    return x * lax.rsqrt(jnp.mean(x * x, axis=-1, keepdims=True) + EPS)


def _adaln_kernel(c_ref, w_ref, b_ref, o_ref):
    h = jax.nn.silu(c_ref[...]).astype(BF16)
    o_ref[...] = _dot(h, w_ref[0].astype(BF16)) + b_ref[0]


def _adaln(cond, w, b, layer):
    rows, d = cond.shape
    depth, _, n = w.shape
    tn = 1536
    return pl.pallas_call(
        _adaln_kernel,
        grid=(n // tn,),
        in_specs=[pl.BlockSpec((rows, d), lambda j: (0, 0)),
                  pl.BlockSpec((1, d, tn), lambda j: (layer, 0, j)),
                  pl.BlockSpec((1, 1, tn), lambda j: (layer, 0, j))],
        out_specs=pl.BlockSpec((rows, tn), lambda j: (0, j)),
        out_shape=jax.ShapeDtypeStruct((rows, n), F32),
        compiler_params=_params("arbitrary"),
        name="adaln",
    )(cond, w, b.reshape(depth, 1, n))


def _gelu_tanh(x):
    c = float(np.sqrt(2.0 / np.pi))
    hx = 0.5 * x
    return hx + hx * jnp.tanh(x * (c + (c * 0.044715) * (x * x)))


def _head_norm_rope_t(p, rope):
    r = lax.rsqrt(jnp.mean(p * p, axis=0, keepdims=True) + EPS)
    h = ROT_HALF
    psw = jnp.concatenate([p[h:2 * h], p[0:h], p[3 * h:4 * h], p[2 * h:3 * h]], axis=0)
    return (p * rope[0] + psw * rope[1]) * r


def _in_proj_kernel(x_ref, sh_ref, sc_ref, g1_ref, wt_ref, wn_ref, rq_ref, rk_ref,
                    cs_ref, vg_ref, ws_ref, bs_ref,
                    q_ref, k_ref, v_ref, g_ref, gm_ref, *, n_q_heads, n_kv_heads):
    tm = x_ref.shape[1]
    sub = KEY_CHUNK
    d = HEAD_DIM
    fw = cs_ref.shape[0]
    gw = vg_ref.shape[1]
    n_heads = gw // GMLP_HEAD
    gain = g1_ref[...] * (1.0 + sc_ref[0])
    shift = sh_ref[0]
    head_of_lane = lax.broadcasted_iota(jnp.int32, (1, gw), 1) // GMLP_HEAD

    def project(j):
        h = (_rms_rows(x_ref[0, j * sub:(j + 1) * sub, :]) * gain + shift).astype(BF16)
        return _dot_nt(wt_ref[...], h), _dot(h, wn_ref[...])

    def finish(j, pt, pn):
        tok = slice(j * sub, (j + 1) * sub)
        rq = rq_ref[:, :, tok]
        rk = rk_ref[:, :, tok]
        for i in range(n_q_heads):
            q_ref[0, i * d:(i + 1) * d, tok] = _head_norm_rope_t(pt[i * d:(i + 1) * d], rq).astype(BF16)
        ko = n_q_heads * d
        for i in range(n_kv_heads):
            kh = _head_norm_rope_t(pt[ko + i * d:ko + (i + 1) * d], rk)
            k_ref[0, i, tok, :] = jnp.concatenate([kh, jnp.zeros_like(kh)], axis=0).T.astype(BF16)
        vo = ko + n_kv_heads * d
        v_ref[0, j] = pt[vo:vo + n_kv_heads * d].astype(BF16)

        fcs = _dot(pn[:, :fw].astype(BF16), cs_ref[...])
        g_ref[0, 0, tok, :] = fcs[:, :fw].astype(BF16)
        g_ref[0, 1, tok, :] = fcs[:, fw:].astype(BF16)

        u = _gelu_tanh(pn[:, fw:fw + gw])
        gv = _gelu_tanh(pn[:, fw + gw:fw + 2 * gw])
        sq = gv * gv
        rinv = jnp.zeros_like(gv)
        for i in range(n_heads):
            m = head_of_lane == i
            ms = jnp.sum(jnp.where(m, sq, 0.0), axis=-1, keepdims=True) * (1.0 / GMLP_HEAD)
            rinv = jnp.where(m, lax.rsqrt(ms + EPS), rinv)
        vn = (gv * rinv * vg_ref[...]).astype(BF16)
        bs = bs_ref[...]
        for c in range(sub // CHUNK):
            vc = vn[c * CHUNK:(c + 1) * CHUNK]
            sp = bs
            for i in range(n_heads):
                sp = sp + jnp.where(head_of_lane == i, _dot(ws_ref[i], vc), 0.0)
            rows = slice(j * sub + c * CHUNK, j * sub + (c + 1) * CHUNK)
            gm_ref[0, rows, :] = (u[c * CHUNK:(c + 1) * CHUNK] * sp).astype(BF16)

    nxt = project(0)
    for j in range(tm // sub):
        cur = nxt
        if j + 1 < tm // sub:
            nxt = project(j + 1)
        finish(j, *cur)


def _in_proj(x, shift, scale, g1, wt, wn, rq, rk, cs, vg, ws, bs, *, tm):
    b, n, d = x.shape
    n_kv = 2
    n_q = n_kv * GQA_GROUP
    qw = n_q * HEAD_DIM
    kw = n_kv * HEAD_DIM
    fw = cs.shape[0]
    gw = vg.shape[1]
    cpt = tm // KEY_CHUNK
    const2 = lambda bi, i: (0, 0)
    const3 = lambda bi, i: (0, 0, 0)
    kern = functools.partial(_in_proj_kernel, n_q_heads=n_q, n_kv_heads=n_kv)
    return pl.pallas_call(
        kern,
        grid=(b, n // tm),
        in_specs=[pl.BlockSpec((1, tm, d), lambda bi, i: (bi, i, 0)),
                  pl.BlockSpec((1, 1, d), lambda bi, i: (bi, 0, 0)),
                  pl.BlockSpec((1, 1, d), lambda bi, i: (bi, 0, 0)),
                  pl.BlockSpec((1, d), const2),
                  pl.BlockSpec(wt.shape, const2),
                  pl.BlockSpec(wn.shape, const2),
                  pl.BlockSpec((2, HEAD_DIM, tm), lambda bi, i: (0, 0, i)),
                  pl.BlockSpec((2, HEAD_DIM, tm), lambda bi, i: (0, 0, i)),
                  pl.BlockSpec(cs.shape, const2),
                  pl.BlockSpec((1, gw), const2),
                  pl.BlockSpec(ws.shape, const3),
                  pl.BlockSpec(bs.shape, const2)],
        out_specs=[pl.BlockSpec((1, qw, tm), lambda bi, i: (bi, 0, i)),
                   pl.BlockSpec((1, n_kv, tm, 2 * HEAD_DIM), lambda bi, i: (bi, 0, i, 0)),
                   pl.BlockSpec((1, cpt, kw, KEY_CHUNK), lambda bi, i: (bi, i, 0, 0)),
                   pl.BlockSpec((1, 2, tm, fw), lambda bi, i: (bi, 0, i, 0)),
                   pl.BlockSpec((1, tm, gw), lambda bi, i: (bi, i, 0))],
        out_shape=[jax.ShapeDtypeStruct((b, qw, n), BF16),
                   jax.ShapeDtypeStruct((b, n_kv, n, 2 * HEAD_DIM), BF16),
                   jax.ShapeDtypeStruct((b, n // KEY_CHUNK, kw, KEY_CHUNK), BF16),
                   jax.ShapeDtypeStruct((b, 2, n, fw), BF16),
                   jax.ShapeDtypeStruct((b, n, gw), BF16)],
        compiler_params=_params("arbitrary", "arbitrary"),
        name="in_proj",
    )(x, shift, scale, g1, wt, wn, rq, rk, cs, vg, ws, bs)


def _attn_queries(q_ref, cols):
    d = HEAD_DIM
    qcat = jnp.concatenate([q_ref[0, g * d:(g + 1) * d, cols] for g in range(GQA_GROUP)], axis=1)
    return jnp.concatenate([qcat, jnp.zeros_like(qcat)], axis=0)


def _attn_finish(acc, o_ref, rows):
    d = HEAD_DIM
    tq = acc.shape[1] // GQA_GROUP
    out = acc[:d] * (1.0 / acc[d:d + 1])
    for j in range(GQA_GROUP // 2):
        pair = jnp.concatenate([out[:, (2 * j) * tq:(2 * j + 1) * tq],
                                out[:, (2 * j + 1) * tq:(2 * j + 2) * tq]], axis=0)
        o_ref[0, rows, 2 * j * d:(2 * j + 2) * d] = pair.T.astype(BF16)


def _attn_kernel_bounded(q_ref, k_ref, v_ref, o_ref, *, n_chunks, tq):
    ones = jnp.ones((16, KEY_CHUNK), BF16)
    n_sub = q_ref.shape[2] // tq
    sub = lambda i: slice(i * tq, (i + 1) * tq)
    qcats = [_attn_queries(q_ref, sub(i)) for i in range(n_sub)]
    work = [(i, c) for i in range(n_sub) for c in range(n_chunks)]
    scores = lambda i, c: _dot(k_ref[0, 0, c * KEY_CHUNK:(c + 1) * KEY_CHUNK, :], qcats[i])
    s_next = scores(*work[0])
    acc = None
    for w, (i, c) in enumerate(work):
        s = s_next
        if w + 1 < len(work):
            s_next = scores(*work[w + 1])
        pf = jnp.exp2(s)
        ps = jnp.sum(pf.reshape(KEY_CHUNK // 8, 8, pf.shape[1]), axis=0)
        pv = _dot(v_ref[0, c], pf.astype(BF16))
        acc, den = (pv, ps) if c == 0 else (acc + pv, den + ps)
        if c == n_chunks - 1:
            full = jnp.concatenate([acc, jnp.broadcast_to(jnp.sum(den, axis=0, keepdims=True), den.shape)], axis=0)
            _attn_finish(full, o_ref, sub(i))


def _attn_kernel_online(q_ref, k_ref, v_ref, o_ref, *, n_chunks, tq):
    assert q_ref.shape[2] == tq
    qcat = _attn_queries(q_ref, slice(0, tq))
    ones = jnp.ones((16, KEY_CHUNK), BF16)

    def step(c, carry):
        m_old, acc = carry
        start = pl.multiple_of(c * KEY_CHUNK, KEY_CHUNK)
        s = _dot(k_ref[0, 0, pl.ds(start, KEY_CHUNK), :], qcat)
        m_new = jnp.maximum(m_old, jnp.max(s, axis=0, keepdims=True))
        alpha = jnp.exp2(m_old - m_new)
        p = jnp.exp2((s - m_new).astype(BF16))
        va = jnp.concatenate([v_ref[0, c], ones], axis=0)
        return m_new, acc * alpha + _dot(va, p)

    m0 = jnp.full((1, qcat.shape[1]), -1e30, F32)
    acc0 = jnp.zeros((HEAD_DIM + 16, qcat.shape[1]), F32)
    _, acc = lax.fori_loop(0, n_chunks, step, (m0, acc0))
    _attn_finish(acc, o_ref, slice(0, tq))


def _attention(qt, k, vt, logit_bound, *, tq):
    n_sub = max(m for m in (1, 2, 4) if qt.shape[2] % (m * tq) == 0)
    bounded = functools.partial(_attention_call, body=_attn_kernel_bounded, tq=tq, n_sub=n_sub)
    online = functools.partial(_attention_call, body=_attn_kernel_online, tq=tq, n_sub=1)
    return lax.cond(logit_bound <= MAX_UNSHIFTED_LOGIT, bounded, online, qt, k, vt)


def _attention_call(qt, k, vt, *, body, tq, n_sub):
    b, qw, s = qt.shape
    n_kv, t, kw = k.shape[1], k.shape[2], k.shape[3]
    n_chunks = t // KEY_CHUNK
    gq = GQA_GROUP * HEAD_DIM
    tb = n_sub * tq
    kern = functools.partial(body, n_chunks=n_chunks, tq=tq)
    return pl.pallas_call(
        kern,
        grid=(b, n_kv, s // tb),
        in_specs=[pl.BlockSpec((1, gq, tb), lambda bi, kv, i: (bi, kv, i)),
                  pl.BlockSpec((1, 1, t, kw), lambda bi, kv, i: (bi, kv, 0, 0)),
                  pl.BlockSpec((1, n_chunks, HEAD_DIM, KEY_CHUNK), lambda bi, kv, i: (bi, 0, kv, 0))],
        out_specs=pl.BlockSpec((1, tb, gq), lambda bi, kv, i: (bi, i, kv)),
        out_shape=jax.ShapeDtypeStruct((b, s, qw), BF16),
        compiler_params=_params("arbitrary", "arbitrary", "arbitrary"),
        name="attention",
    )(qt, k, vt)


def _fourier_kernel(chi_ref, shi_ref, clo_ref, slo_ref, mirror_ref, g_ref, lo_ref, hi_ref, wc_scr, ws_scr,
                    *, scale, hi_per_tile):
    n = clo_ref.shape[1]
    tm = hi_per_tile * DFT_RADIX
    i = pl.program_id(0)

    @pl.when(pl.program_id(1) == 0)
    def _():
        clo = clo_ref[...]
        slo = slo_ref[...]
        for j in range(hi_per_tile):
            chi = chi_ref[pl.ds(i * hi_per_tile + j, 1), :]
            shi = shi_ref[pl.ds(i * hi_per_tile + j, 1), :]
            rows = slice(j * DFT_RADIX, (j + 1) * DFT_RADIX)
            wc_scr[rows, :] = (chi * clo - shi * slo).astype(BF16)
            ws_scr[rows, :] = (shi * clo + chi * slo).astype(BF16)
        nxt = (i + 1) * hi_per_tile
        pad = wc_scr.shape[0] - tm
        wc_scr[tm:, :] = jnp.broadcast_to(chi_ref[pl.ds(nxt, 1), :], (pad, n)).astype(BF16)
        ws_scr[tm:, :] = jnp.broadcast_to(shi_ref[pl.ds(nxt, 1), :], (pad, n)).astype(BF16)

    yc = _dot(wc_scr[...], g_ref[0, 0:n, :])
    ys = _dot(ws_scr[...], g_ref[0, n:2 * n, :])
    lo_ref[0] = ((yc[:tm] - ys[:tm]) * scale).astype(BF16)
    hi_ref[0] = _dot(mirror_ref[...], ((yc + ys) * scale).astype(BF16)).astype(BF16)


def _dft_tables(n):
    hi = n // DFT_RADIX
    t = np.arange(n, dtype=np.int64)
    a_hi = (DFT_RADIX * np.arange(hi, dtype=np.int64)[:, None] * t[None, :]) % n
    a_lo = (np.arange(DFT_RADIX, dtype=np.int64)[:, None] * t[None, :]) % n
    f = lambda fn, a: jnp.asarray(fn(2.0 * np.pi * a / n), dtype=F32)
    return f(np.cos, a_hi), f(np.sin, a_hi), f(np.cos, a_lo), f(np.sin, a_lo)


def _fourier(g):
    b, n2, fw = g.shape
    n = n2 // 2
    chi, shi, clo, slo = _dft_tables(n)
    n_hi = n // DFT_RADIX
    hi_per_tile = min(8, n_hi // 2)
    tmf = hi_per_tile * DFT_RADIX
    n_tiles = n // (2 * tmf)
    rows_ext = tmf + 16
    mirror = np.zeros((tmf, rows_ext), np.float32)
    mirror[np.arange(tmf), tmf - np.arange(tmf)] = 1.0
    mirror = jnp.asarray(mirror).astype(BF16)
    const = lambda a: pl.BlockSpec(a.shape, lambda i, bi: (0, 0))
    kern = functools.partial(_fourier_kernel, scale=float(1.0 / np.sqrt(n * FOURIER_GROUP)),
                             hi_per_tile=hi_per_tile)
    half = jax.ShapeDtypeStruct((b, n // 2, fw), BF16)
    lo, hi = pl.pallas_call(
        kern,
        grid=(n_tiles, b),
        in_specs=[const(chi), const(shi), const(clo), const(slo), const(mirror),
                  pl.BlockSpec((1, n2, fw), lambda i, bi: (bi, 0, 0))],
        out_specs=[pl.BlockSpec((1, tmf, fw), lambda i, bi: (bi, i, 0)),
                   pl.BlockSpec((1, tmf, fw), lambda i, bi: (bi, n_tiles - 1 - i, 0))],
        out_shape=[half, half],
        scratch_shapes=[pltpu.VMEM((rows_ext, n), BF16), pltpu.VMEM((rows_ext, n), BF16)],
        compiler_params=_params("arbitrary", "arbitrary"),
        name="fourier",
    )(chi, shi, clo, slo, mirror, g)
    return jnp.concatenate([lo, hi], axis=1)


def _out_mlp_kernel(x_ref, a_ref, f_ref, m_ref, wo_ref, g2_ref, gt1_ref, sh_ref, sc_ref, gt2_ref,
                    w1_ref, w2_ref, o_ref, *, ff_chunk):
    aw = a_ref.shape[2]
    fw = f_ref.shape[2]
    mix = (_dot(a_ref[0], wo_ref[0:aw]) + _dot(f_ref[0], wo_ref[aw:aw + fw])
           + _dot(m_ref[0], wo_ref[aw + fw:]))
    x1 = x_ref[0] + gt1_ref[0] * mix
    h = ((_rms_rows(x1) * g2_ref[...]) * (1.0 + sc_ref[0]) + sh_ref[0]).astype(BF16)
    y = jnp.zeros_like(x1)
    for j in range(w1_ref.shape[1] // ff_chunk):
        cols = slice(j * ff_chunk, (j + 1) * ff_chunk)
        a = jnp.square(jnp.maximum(_dot(h, w1_ref[:, cols]), 0.0)).astype(BF16)
        y = y + _dot(a, w2_ref[cols, :])
    o_ref[0] = x1 + gt2_ref[0] * y


def _out_mlp(x, attn, four, gm, wo, g2, gt1, shift, scale, gt2, w1, w2, *, tm):
    b, n, d = x.shape
    tok = lambda w: pl.BlockSpec((1, tm, w), lambda bi, i: (bi, i, 0))
    per_b = pl.BlockSpec((1, 1, d), lambda bi, i: (bi, 0, 0))
    const = lambda a: pl.BlockSpec(a.shape, lambda bi, i: (0, 0), pipeline_mode=pl.Buffered(1))
    kern = functools.partial(_out_mlp_kernel, ff_chunk=1024)
    return pl.pallas_call(
        kern,
        grid=(b, n // tm),
        in_specs=[tok(d), tok(attn.shape[2]), tok(four.shape[2]), tok(gm.shape[2]),
                  const(wo), pl.BlockSpec((1, d), lambda bi, i: (0, 0)),
                  per_b, per_b, per_b, per_b, const(w1), const(w2)],
        out_specs=tok(d),
        out_shape=jax.ShapeDtypeStruct((b, n, d), F32),
        compiler_params=_params("arbitrary", "arbitrary"),
        name="out_mlp",
    )(x, attn, four, gm, wo, g2, gt1, shift, scale, gt2, w1, w2)


def _rope_tables_t(n):
    rows = n // GRID_W
    row = jnp.repeat(jnp.arange(rows, dtype=F32), GRID_W)
    col = jnp.tile(jnp.arange(GRID_W, dtype=F32), rows)
    inv = ROPE_THETA ** (-jnp.arange(0, AXIS_DIM, 2, dtype=F32) / AXIS_DIM)
    ar = (row[:, None] * inv).T
    ac = (col[:, None] * inv).T
    ct = jnp.concatenate([jnp.cos(ar), jnp.cos(ar), jnp.cos(ac), jnp.cos(ac)], axis=0)
    st = jnp.concatenate([-jnp.sin(ar), jnp.sin(ar), -jnp.sin(ac), jnp.sin(ac)], axis=0)
    return ct, st


def _gained_rope(ct, st, gain, scale):
    h = ROT_HALF
    gain_sw = jnp.concatenate([gain[h:2 * h], gain[0:h], gain[3 * h:4 * h], gain[2 * h:3 * h]])
    return jnp.stack([ct * (gain * scale)[:, None], st * (gain_sw * scale)[:, None]])


def _channel_dft():
    c = np.arange(FOURIER_GROUP)
    ang = 2.0 * np.pi * ((c[:, None] * c[None, :]) % FOURIER_GROUP) / FOURIER_GROUP
    return np.cos(ang), np.sin(ang)


def kernel(x, c, ctx, c_ctx, w_ada, b_ada, norm1_g, norm2_g, w_in, q_norm_g, k_norm_g, gmlp_v_g,
           w_spatial, b_spatial, w_out, w_mlp1, w_mlp2):
    depth = w_ada.shape[0]
    b, s, d = x.shape
    n_ctx = ctx.shape[1]
    gw = gmlp_v_g.shape[1]
    n_gmlp_heads = gw // GMLP_HEAD
    fw = d // 4
    qkv_w = w_in.shape[2] - fw - 2 * gw

    cond = jnp.zeros((16, d), F32).at[:b].set(c).at[b].set(c_ctx)
    ct_x, st_x = _rope_tables_t(s)
    ct_c = jnp.ones((HEAD_DIM, n_ctx), F32)
    st_c = jnp.zeros((HEAD_DIM, n_ctx), F32)
    cc, sc = _channel_dft()
    eye = np.eye(fw // FOURIER_GROUP)
    cs = jnp.asarray(np.concatenate([np.kron(eye, cc), np.kron(eye, sc)], axis=1), dtype=F32).astype(BF16)

    tm_x = 512 if s % 512 == 0 else KEY_CHUNK
    tm_c = KEY_CHUNK

    for l in range(depth):
        mod = _adaln(cond, w_ada, b_ada, l)
        mx = [m[:, None, :] for m in jnp.split(mod[:b], 6, axis=-1)]
        mc = [jnp.broadcast_to(m[None, :, :], (b, 1, d)) for m in jnp.split(mod[b:b + 1], 6, axis=-1)]

        wt = w_in[l][:, :qkv_w].T.astype(BF16)
        wn = w_in[l][:, qkv_w:].astype(BF16)
        wts = dict(g1=norm1_g[l].reshape(1, d), wt=wt, wn=wn,
                   cs=cs, vg=gmlp_v_g[l].reshape(1, gw), ws=w_spatial[l].astype(BF16),
                   bs=jnp.repeat(b_spatial[l].T, GMLP_HEAD, axis=1))
        rope_x = dict(rq=_gained_rope(ct_x, st_x, q_norm_g[l], Q_SCALE), rk=_gained_rope(ct_x, st_x, k_norm_g[l], 1.0))
        rope_c = dict(rq=_gained_rope(ct_c, st_c, q_norm_g[l], Q_SCALE), rk=_gained_rope(ct_c, st_c, k_norm_g[l], 1.0))
        q_x, k_x, v_x, g_x, gm_x = _in_proj(x, mx[0], mx[1], tm=tm_x, **rope_x, **wts)
        q_c, k_c, v_c, g_c, gm_c = _in_proj(ctx, mc[0], mc[1], tm=tm_c, **rope_c, **wts)

        logit_bound = (HEAD_DIM * Q_SCALE) * jnp.max(jnp.abs(q_norm_g[l])) * jnp.max(jnp.abs(k_norm_g[l]))
        k_all = jnp.concatenate([k_c, k_x], axis=2)
        v_all = jnp.concatenate([v_c, v_x], axis=1)
        a_x = _attention(q_x, k_all, v_all, logit_bound, tq=KEY_CHUNK)
        f_x = _fourier(g_x.reshape(b, 2 * s, fw))

        wo = w_out[l].astype(BF16)
        w1 = w_mlp1[l].astype(BF16)
        w2 = w_mlp2[l].astype(BF16)
        g2 = norm2_g[l].reshape(1, d)
        x = _out_mlp(x, a_x, f_x, gm_x, wo, g2, mx[2], mx[3], mx[4], mx[5], w1, w2, tm=tm_x)

        if l < depth - 1:
            a_c = _attention(q_c, k_c, v_c, logit_bound, tq=KEY_CHUNK)
            f_c = _fourier(g_c.reshape(b, 2 * n_ctx, fw))
            ctx = _out_mlp(ctx, a_c, f_c, gm_c, wo, g2, mc[2], mc[3], mc[4], mc[5], w1, w2, tm=tm_c)
    return x
```

```python
import functools

import numpy as np
import jax
import jax.numpy as jnp
from jax import lax
from jax.experimental import pallas as pl
from jax.experimental.pallas import tpu as pltpu

GRID_W = 64
HEAD_DIM = 64
AXIS_DIM = HEAD_DIM // 2
ROT_HALF = AXIS_DIM // 2
GQA_GROUP = 4
FOURIER_GROUP = 64
GMLP_HEAD = 64
CHUNK = 128
ROPE_THETA = 10000.0
EPS = 1e-6

VMEM_LIMIT_BYTES = 56 * 1024 * 1024
KEY_CHUNK = 256
DFT_RADIX = 64
LOG2_E = 1.4426950408889634
Q_SCALE = HEAD_DIM ** -0.5 * LOG2_E
MAX_UNSHIFTED_LOGIT = 60.0

BF16 = jnp.bfloat16
F32 = jnp.float32


def _params(*sem):
    return pltpu.CompilerParams(dimension_semantics=sem, vmem_limit_bytes=VMEM_LIMIT_BYTES)


def _dot(a, b):
    return jnp.dot(a, b, preferred_element_type=F32)


def _dot_nt(a, b):
    return lax.dot_general(a, b, (((1,), (1,)), ((), ())), preferred_element_type=F32)


def _rms_rows(x):
    return x * lax.rsqrt(jnp.mean(x * x, axis=-1, keepdims=True) + EPS)


def _adaln_kernel(c_ref, w_ref, b_ref, o_ref):
    h = jax.nn.silu(c_ref[...]).astype(BF16)
    o_ref[...] = _dot(h, w_ref[0].astype(BF16)) + b_ref[0]


def _adaln(cond, w, b, layer):
    rows, d = cond.shape
    depth, _, n = w.shape
    tn = 1536
    return pl.pallas_call(
        _adaln_kernel,
        grid=(n // tn,),
        in_specs=[pl.BlockSpec((rows, d), lambda j: (0, 0)),
                  pl.BlockSpec((1, d, tn), lambda j: (layer, 0, j)),
                  pl.BlockSpec((1, 1, tn), lambda j: (layer, 0, j))],
        out_specs=pl.BlockSpec((rows, tn), lambda j: (0, j)),
        out_shape=jax.ShapeDtypeStruct((rows, n), F32),
        compiler_params=_params("arbitrary"),
        name="adaln",
    )(cond, w, b.reshape(depth, 1, n))


def _gelu_tanh(x):
    c = float(np.sqrt(2.0 / np.pi))
    hx = 0.5 * x
    return hx + hx * jnp.tanh(x * (c + (c * 0.044715) * (x * x)))


def _head_norm_rope_t(p, rope):
    r = lax.rsqrt(jnp.mean(p * p, axis=0, keepdims=True) + EPS)
    h = ROT_HALF
    psw = jnp.concatenate([p[h:2 * h], p[0:h], p[3 * h:4 * h], p[2 * h:3 * h]], axis=0)
    return (p * rope[0] + psw * rope[1]) * r


def _in_proj_kernel(x_ref, sh_ref, sc_ref, g1_ref, wt_ref, wn_ref, rq_ref, rk_ref,
                    cs_ref, vg_ref, ws_ref, bs_ref,
                    q_ref, k_ref, v_ref, g_ref, gm_ref, *, n_q_heads, n_kv_heads):
    tm = x_ref.shape[1]
    sub = KEY_CHUNK
    d = HEAD_DIM
    fw = cs_ref.shape[0]
    gw = vg_ref.shape[1]
    n_heads = gw // GMLP_HEAD
    gain = g1_ref[...] * (1.0 + sc_ref[0])
    shift = sh_ref[0]
    head_of_lane = lax.broadcasted_iota(jnp.int32, (1, gw), 1) // GMLP_HEAD

    def project(j):
        h = (_rms_rows(x_ref[0, j * sub:(j + 1) * sub, :]) * gain + shift).astype(BF16)
        return _dot_nt(wt_ref[...], h), _dot(h, wn_ref[...])

    def finish(j, pt, pn):
        tok = slice(j * sub, (j + 1) * sub)
        rq = rq_ref[:, :, tok]
        rk = rk_ref[:, :, tok]
        for i in range(n_q_heads):
            q_ref[0, i * d:(i + 1) * d, tok] = _head_norm_rope_t(pt[i * d:(i + 1) * d], rq).astype(BF16)
        ko = n_q_heads * d
        for i in range(n_kv_heads):
            kh = _head_norm_rope_t(pt[ko + i * d:ko + (i + 1) * d], rk)
            k_ref[0, i, tok, :] = jnp.concatenate([kh, jnp.zeros_like(kh)], axis=0).T.astype(BF16)
        vo = ko + n_kv_heads * d
        v_ref[0, j] = pt[vo:vo + n_kv_heads * d].astype(BF16)

        fcs = _dot(pn[:, :fw].astype(BF16), cs_ref[...])
        g_ref[0, 0, tok, :] = fcs[:, :fw].astype(BF16)
        g_ref[0, 1, tok, :] = fcs[:, fw:].astype(BF16)

        u = _gelu_tanh(pn[:, fw:fw + gw])
        gv = _gelu_tanh(pn[:, fw + gw:fw + 2 * gw])
        sq = gv * gv
        rinv = jnp.zeros_like(gv)
        for i in range(n_heads):
            m = head_of_lane == i
            ms = jnp.sum(jnp.where(m, sq, 0.0), axis=-1, keepdims=True) * (1.0 / GMLP_HEAD)
            rinv = jnp.where(m, lax.rsqrt(ms + EPS), rinv)
        vn = (gv * rinv * vg_ref[...]).astype(BF16)
        bs = bs_ref[...]
        for c in range(sub // CHUNK):
            vc = vn[c * CHUNK:(c + 1) * CHUNK]
            sp = bs
            for i in range(n_heads):
                sp = sp + jnp.where(head_of_lane == i, _dot(ws_ref[i], vc), 0.0)
            rows = slice(j * sub + c * CHUNK, j * sub + (c + 1) * CHUNK)
            gm_ref[0, rows, :] = (u[c * CHUNK:(c + 1) * CHUNK] * sp).astype(BF16)

    nxt = project(0)
    for j in range(tm // sub):
        cur = nxt
        if j + 1 < tm // sub:
            nxt = project(j + 1)
        finish(j, *cur)


def _in_proj(x, shift, scale, g1, wt, wn, rq, rk, cs, vg, ws, bs, *, tm):
    b, n, d = x.shape
    n_kv = 2
    n_q = n_kv * GQA_GROUP
    qw = n_q * HEAD_DIM
    kw = n_kv * HEAD_DIM
    fw = cs.shape[0]
    gw = vg.shape[1]
    cpt = tm // KEY_CHUNK
    const2 = lambda bi, i: (0, 0)
    const3 = lambda bi, i: (0, 0, 0)
    kern = functools.partial(_in_proj_kernel, n_q_heads=n_q, n_kv_heads=n_kv)
    return pl.pallas_call(
        kern,
        grid=(b, n // tm),
        in_specs=[pl.BlockSpec((1, tm, d), lambda bi, i: (bi, i, 0)),
                  pl.BlockSpec((1, 1, d), lambda bi, i: (bi, 0, 0)),
                  pl.BlockSpec((1, 1, d), lambda bi, i: (bi, 0, 0)),
                  pl.BlockSpec((1, d), const2),
                  pl.BlockSpec(wt.shape, const2),
                  pl.BlockSpec(wn.shape, const2),
                  pl.BlockSpec((2, HEAD_DIM, tm), lambda bi, i: (0, 0, i)),
                  pl.BlockSpec((2, HEAD_DIM, tm), lambda bi, i: (0, 0, i)),
                  pl.BlockSpec(cs.shape, const2),
                  pl.BlockSpec((1, gw), const2),
                  pl.BlockSpec(ws.shape, const3),
                  pl.BlockSpec(bs.shape, const2)],
        out_specs=[pl.BlockSpec((1, qw, tm), lambda bi, i: (bi, 0, i)),
                   pl.BlockSpec((1, n_kv, tm, 2 * HEAD_DIM), lambda bi, i: (bi, 0, i, 0)),
                   pl.BlockSpec((1, cpt, kw, KEY_CHUNK), lambda bi, i: (bi, i, 0, 0)),
                   pl.BlockSpec((1, 2, tm, fw), lambda bi, i: (bi, 0, i, 0)),
                   pl.BlockSpec((1, tm, gw), lambda bi, i: (bi, i, 0))],
        out_shape=[jax.ShapeDtypeStruct((b, qw, n), BF16),
                   jax.ShapeDtypeStruct((b, n_kv, n, 2 * HEAD_DIM), BF16),
                   jax.ShapeDtypeStruct((b, n // KEY_CHUNK, kw, KEY_CHUNK), BF16),
                   jax.ShapeDtypeStruct((b, 2, n, fw), BF16),
                   jax.ShapeDtypeStruct((b, n, gw), BF16)],
        compiler_params=_params("arbitrary", "arbitrary"),
        name="in_proj",
    )(x, shift, scale, g1, wt, wn, rq, rk, cs, vg, ws, bs)


def _attn_queries(q_ref, cols):
    d = HEAD_DIM
    qcat = jnp.concatenate([q_ref[0, g * d:(g + 1) * d, cols] for g in range(GQA_GROUP)], axis=1)
    return jnp.concatenate([qcat, jnp.zeros_like(qcat)], axis=0)


def _attn_finish(acc, o_ref, rows):
    d = HEAD_DIM
    tq = acc.shape[1] // GQA_GROUP
    out = acc[:d] * (1.0 / acc[d:d + 1])
    for j in range(GQA_GROUP // 2):
        pair = jnp.concatenate([out[:, (2 * j) * tq:(2 * j + 1) * tq],
                                out[:, (2 * j + 1) * tq:(2 * j + 2) * tq]], axis=0)
        o_ref[0, rows, 2 * j * d:(2 * j + 2) * d] = pair.T.astype(BF16)


def _attn_sources(refs):
    n_src = (len(refs) - 1) // 2
    return refs[:n_src], refs[n_src:2 * n_src], refs[2 * n_src]


def _attn_kernel_bounded(q_ref, *refs, tq):
    k_refs, v_refs, o_ref = _attn_sources(refs)
    chunks = [(j, c) for j, kr in enumerate(k_refs) for c in range(kr.shape[2] // KEY_CHUNK)]
    n_sub = q_ref.shape[2] // tq
    sub = lambda i: slice(i * tq, (i + 1) * tq)
    heads = range(GQA_GROUP)
    qcats = [_attn_queries(q_ref, sub(i)) for i in range(n_sub)]
    work = [(i, jc) for i in range(n_sub) for jc in chunks]
    scores = lambda i, jc, g: _dot(k_refs[jc[0]][0, 0, jc[1] * KEY_CHUNK:(jc[1] + 1) * KEY_CHUNK, :],
                                   qcats[i][:, g * tq:(g + 1) * tq])
    s_next = [scores(*work[0], g) for g in heads]
    acc = den = None
    for w, (i, jc) in enumerate(work):
        s = s_next
        pf = [jnp.exp2(s[g]) for g in heads]
        ps = [jnp.sum(pf[g].reshape(KEY_CHUNK // 8, 8, tq), axis=0) for g in heads]
        s_next, pv = [], []
        for g in heads:
            if w + 1 < len(work):
                s_next.append(scores(*work[w + 1], g))
            pv.append(_dot(v_refs[jc[0]][0, jc[1]], pf[g].astype(BF16)))
        acc = pv if jc == chunks[0] else [acc[g] + pv[g] for g in heads]
        den = ps if jc == chunks[0] else [den[g] + ps[g] for g in heads]
        if jc == chunks[-1]:
            total = [jnp.broadcast_to(jnp.sum(den[g], axis=0, keepdims=True), den[g].shape) for g in heads]
            full = jnp.concatenate([jnp.concatenate(acc, axis=1), jnp.concatenate(total, axis=1)], axis=0)
            _attn_finish(full, o_ref, sub(i))


def _attn_kernel_online(q_ref, *refs, tq):
    k_refs, v_refs, o_ref = _attn_sources(refs)
    assert q_ref.shape[2] == tq
    qcat = _attn_queries(q_ref, slice(0, tq))
    ones = jnp.ones((16, KEY_CHUNK), BF16)
    carry = (jnp.full((1, qcat.shape[1]), -1e30, F32), jnp.zeros((HEAD_DIM + 16, qcat.shape[1]), F32))
    for k_ref, v_ref in zip(k_refs, v_refs):
        def step(c, carry, k_ref=k_ref, v_ref=v_ref):
            m_old, acc = carry
            start = pl.multiple_of(c * KEY_CHUNK, KEY_CHUNK)
            s = _dot(k_ref[0, 0, pl.ds(start, KEY_CHUNK), :], qcat)
            m_new = jnp.maximum(m_old, jnp.max(s, axis=0, keepdims=True))
            alpha = jnp.exp2(m_old - m_new)
            p = jnp.exp2((s - m_new).astype(BF16))
            va = jnp.concatenate([v_ref[0, c], ones], axis=0)
            return m_new, acc * alpha + _dot(va, p)

        carry = lax.fori_loop(0, k_ref.shape[2] // KEY_CHUNK, step, carry)
    _attn_finish(carry[1], o_ref, slice(0, tq))


def _attention(qt, ks, vs, logit_bound, *, tq):
    n_sub = max(m for m in (1, 2, 4) if qt.shape[2] % (m * tq) == 0)
    bounded = functools.partial(_attention_call, body=_attn_kernel_bounded, tq=tq, n_sub=n_sub)
    online = functools.partial(_attention_call, body=_attn_kernel_online, tq=tq, n_sub=1)
    return lax.cond(logit_bound <= MAX_UNSHIFTED_LOGIT, bounded, online, qt, *ks, *vs)


def _attention_call(qt, *kv, body, tq, n_sub):
    b, qw, s = qt.shape
    ks, vs = kv[:len(kv) // 2], kv[len(kv) // 2:]
    gq = GQA_GROUP * HEAD_DIM
    tb = n_sub * tq
    k_spec = lambda k: pl.BlockSpec((1, 1) + k.shape[2:], lambda bi, kv, i: (bi, kv, 0, 0))
    v_spec = lambda v: pl.BlockSpec((1, v.shape[1], HEAD_DIM, KEY_CHUNK), lambda bi, kv, i: (bi, 0, kv, 0))
    return pl.pallas_call(
        functools.partial(body, tq=tq),
        grid=(b, ks[0].shape[1], s // tb),
        in_specs=[pl.BlockSpec((1, gq, tb), lambda bi, kv, i: (bi, kv, i))]
                 + [k_spec(k) for k in ks] + [v_spec(v) for v in vs],
        out_specs=pl.BlockSpec((1, tb, gq), lambda bi, kv, i: (bi, i, kv)),
        out_shape=jax.ShapeDtypeStruct((b, s, qw), BF16),
        compiler_params=_params("arbitrary", "arbitrary", "arbitrary"),
        name="attention",
    )(qt, *ks, *vs)


def _fourier_kernel(chi_ref, shi_ref, clo_ref, slo_ref, mirror_ref, g_ref, lo_ref, hi_ref, wc_scr, ws_scr,
                    *, scale, hi_per_tile):
    n = clo_ref.shape[1]
    tm = hi_per_tile * DFT_RADIX
    i = pl.program_id(0)

    @pl.when(pl.program_id(1) == 0)
    def _():
        clo = clo_ref[...]
        slo = slo_ref[...]
        for j in range(hi_per_tile):
            chi = chi_ref[pl.ds(i * hi_per_tile + j, 1), :]
            shi = shi_ref[pl.ds(i * hi_per_tile + j, 1), :]
            rows = slice(j * DFT_RADIX, (j + 1) * DFT_RADIX)
            wc_scr[rows, :] = (chi * clo - shi * slo).astype(BF16)
            ws_scr[rows, :] = (shi * clo + chi * slo).astype(BF16)
        nxt = (i + 1) * hi_per_tile
        pad = wc_scr.shape[0] - tm
        wc_scr[tm:, :] = jnp.broadcast_to(chi_ref[pl.ds(nxt, 1), :], (pad, n)).astype(BF16)
        ws_scr[tm:, :] = jnp.broadcast_to(shi_ref[pl.ds(nxt, 1), :], (pad, n)).astype(BF16)

    yc = _dot(wc_scr[...], g_ref[0, 0:n, :])
    ys = _dot(ws_scr[...], g_ref[0, n:2 * n, :])
    lo_ref[0] = ((yc[:tm] - ys[:tm]) * scale).astype(BF16)
    hi_ref[0] = _dot(mirror_ref[...], ((yc + ys) * scale).astype(BF16)).astype(BF16)


def _dft_tables(n):
    hi = n // DFT_RADIX
    t = np.arange(n, dtype=np.int64)
    a_hi = (DFT_RADIX * np.arange(hi, dtype=np.int64)[:, None] * t[None, :]) % n
    a_lo = (np.arange(DFT_RADIX, dtype=np.int64)[:, None] * t[None, :]) % n
    f = lambda fn, a: jnp.asarray(fn(2.0 * np.pi * a / n), dtype=F32)
    return f(np.cos, a_hi), f(np.sin, a_hi), f(np.cos, a_lo), f(np.sin, a_lo)


def _fourier(g):
    b, n2, fw = g.shape
    n = n2 // 2
    chi, shi, clo, slo = _dft_tables(n)
    n_hi = n // DFT_RADIX
    hi_per_tile = min(8, n_hi // 2)
    tmf = hi_per_tile * DFT_RADIX
    n_tiles = n // (2 * tmf)
    rows_ext = tmf + 16
    mirror = np.zeros((tmf, rows_ext), np.float32)
    mirror[np.arange(tmf), tmf - np.arange(tmf)] = 1.0
    mirror = jnp.asarray(mirror).astype(BF16)
    const = lambda a: pl.BlockSpec(a.shape, lambda i, bi: (0, 0))
    kern = functools.partial(_fourier_kernel, scale=float(1.0 / np.sqrt(n * FOURIER_GROUP)),
                             hi_per_tile=hi_per_tile)
    half = jax.ShapeDtypeStruct((b, n // 2, fw), BF16)
    lo, hi = pl.pallas_call(
        kern,
        grid=(n_tiles, b),
        in_specs=[const(chi), const(shi), const(clo), const(slo), const(mirror),
                  pl.BlockSpec((1, n2, fw), lambda i, bi: (bi, 0, 0))],
        out_specs=[pl.BlockSpec((1, tmf, fw), lambda i, bi: (bi, i, 0)),
                   pl.BlockSpec((1, tmf, fw), lambda i, bi: (bi, n_tiles - 1 - i, 0))],
        out_shape=[half, half],
        scratch_shapes=[pltpu.VMEM((rows_ext, n), BF16), pltpu.VMEM((rows_ext, n), BF16)],
        compiler_params=_params("arbitrary", "arbitrary"),
        name="fourier",
    )(chi, shi, clo, slo, mirror, g)
    return lo, hi


def _out_mlp_kernel(x_ref, a_ref, flo_ref, fhi_ref, m_ref, wo_ref, g2_ref, gt1_ref, sh_ref, sc_ref, gt2_ref,
                    w1_ref, w2_ref, o_ref, *, ff_chunk, lo_tiles):
    aw = a_ref.shape[2]
    fw = flo_ref.shape[2]
    if lo_tiles:
        four = jnp.where(pl.program_id(1) < lo_tiles, flo_ref[0], fhi_ref[0])
    else:
        four = jnp.concatenate([flo_ref[0], fhi_ref[0]], axis=0)
    mix = (_dot(a_ref[0], wo_ref[0:aw]) + _dot(four, wo_ref[aw:aw + fw])
           + _dot(m_ref[0], wo_ref[aw + fw:]))
    x1 = x_ref[0] + gt1_ref[0] * mix
    h = ((_rms_rows(x1) * g2_ref[...]) * (1.0 + sc_ref[0]) + sh_ref[0]).astype(BF16)
    y = jnp.zeros_like(x1)
    for j in range(w1_ref.shape[1] // ff_chunk):
        cols = slice(j * ff_chunk, (j + 1) * ff_chunk)
        a = jnp.square(jnp.maximum(_dot(h, w1_ref[:, cols]), 0.0)).astype(BF16)
        y = y + _dot(a, w2_ref[cols, :])
    o_ref[0] = x1 + gt2_ref[0] * y


def _out_mlp(x, attn, four, gm, wo, g2, gt1, shift, scale, gt2, w1, w2, *, tm):
    b, n, d = x.shape
    f_lo, f_hi = four
    half = f_lo.shape[1]
    fw = f_lo.shape[2]
    tok = lambda w: pl.BlockSpec((1, tm, w), lambda bi, i: (bi, i, 0))
    if tm == n:
        lo_tiles = 0
        lo_spec = hi_spec = pl.BlockSpec((1, half, fw), lambda bi, i: (bi, 0, 0))
    else:
        lo_tiles = half // tm
        lo_spec = pl.BlockSpec((1, tm, fw), lambda bi, i: (bi, jnp.minimum(i, lo_tiles - 1), 0))
        hi_spec = pl.BlockSpec((1, tm, fw), lambda bi, i: (bi, jnp.maximum(i - lo_tiles, 0), 0))
    per_b = pl.BlockSpec((1, 1, d), lambda bi, i: (bi, 0, 0))
    const = lambda a: pl.BlockSpec(a.shape, lambda bi, i: (0, 0), pipeline_mode=pl.Buffered(1))
    kern = functools.partial(_out_mlp_kernel, ff_chunk=1024, lo_tiles=lo_tiles)
    return pl.pallas_call(
        kern,
        grid=(b, n // tm),
        in_specs=[tok(d), tok(attn.shape[2]), lo_spec, hi_spec, tok(gm.shape[2]),
                  const(wo), pl.BlockSpec((1, d), lambda bi, i: (0, 0)),
                  per_b, per_b, per_b, per_b, const(w1), const(w2)],
        out_specs=tok(d),
        out_shape=jax.ShapeDtypeStruct((b, n, d), F32),
        compiler_params=_params("arbitrary", "arbitrary"),
        name="out_mlp",
    )(x, attn, f_lo, f_hi, gm, wo, g2, gt1, shift, scale, gt2, w1, w2)


def _rope_tables_t(n):
    rows = n // GRID_W
    row = jnp.repeat(jnp.arange(rows, dtype=F32), GRID_W)
    col = jnp.tile(jnp.arange(GRID_W, dtype=F32), rows)
    inv = ROPE_THETA ** (-jnp.arange(0, AXIS_DIM, 2, dtype=F32) / AXIS_DIM)
    ar = (row[:, None] * inv).T
    ac = (col[:, None] * inv).T
    ct = jnp.concatenate([jnp.cos(ar), jnp.cos(ar), jnp.cos(ac), jnp.cos(ac)], axis=0)
    st = jnp.concatenate([-jnp.sin(ar), jnp.sin(ar), -jnp.sin(ac), jnp.sin(ac)], axis=0)
    return ct, st


def _gained_rope(ct, st, gain, scale):
    h = ROT_HALF
    gain_sw = jnp.concatenate([gain[h:2 * h], gain[0:h], gain[3 * h:4 * h], gain[2 * h:3 * h]])
    return jnp.stack([ct * (gain * scale)[:, None], st * (gain_sw * scale)[:, None]])


def _channel_dft():
    c = np.arange(FOURIER_GROUP)
    ang = 2.0 * np.pi * ((c[:, None] * c[None, :]) % FOURIER_GROUP) / FOURIER_GROUP
    return np.cos(ang), np.sin(ang)


def kernel(x, c, ctx, c_ctx, w_ada, b_ada, norm1_g, norm2_g, w_in, q_norm_g, k_norm_g, gmlp_v_g,
           w_spatial, b_spatial, w_out, w_mlp1, w_mlp2):
    depth = w_ada.shape[0]
    b, s, d = x.shape
    n_ctx = ctx.shape[1]
    gw = gmlp_v_g.shape[1]
    n_gmlp_heads = gw // GMLP_HEAD
    fw = d // 4
    qkv_w = w_in.shape[2] - fw - 2 * gw

    cond = jnp.zeros((16, d), F32).at[:b].set(c).at[b].set(c_ctx)
    ct_x, st_x = _rope_tables_t(s)
    ct_c = jnp.ones((HEAD_DIM, n_ctx), F32)
    st_c = jnp.zeros((HEAD_DIM, n_ctx), F32)
    cc, sc = _channel_dft()
    eye = np.eye(fw // FOURIER_GROUP)
    cs = jnp.asarray(np.concatenate([np.kron(eye, cc), np.kron(eye, sc)], axis=1), dtype=F32).astype(BF16)

    tm_x = 512 if s % 512 == 0 else KEY_CHUNK
    tm_c = KEY_CHUNK

    for l in range(depth):
        mod = _adaln(cond, w_ada, b_ada, l)
        mx = [m[:, None, :] for m in jnp.split(mod[:b], 6, axis=-1)]
        mc = [jnp.broadcast_to(m[None, :, :], (b, 1, d)) for m in jnp.split(mod[b:b + 1], 6, axis=-1)]

        wt = w_in[l][:, :qkv_w].T.astype(BF16)
        wn = w_in[l][:, qkv_w:].astype(BF16)
        wts = dict(g1=norm1_g[l].reshape(1, d), wt=wt, wn=wn,
                   cs=cs, vg=gmlp_v_g[l].reshape(1, gw), ws=w_spatial[l].astype(BF16),
                   bs=jnp.repeat(b_spatial[l].T, GMLP_HEAD, axis=1))
        rope_x = dict(rq=_gained_rope(ct_x, st_x, q_norm_g[l], Q_SCALE), rk=_gained_rope(ct_x, st_x, k_norm_g[l], 1.0))
        rope_c = dict(rq=_gained_rope(ct_c, st_c, q_norm_g[l], Q_SCALE), rk=_gained_rope(ct_c, st_c, k_norm_g[l], 1.0))
        q_x, k_x, v_x, g_x, gm_x = _in_proj(x, mx[0], mx[1], tm=tm_x, **rope_x, **wts)
        q_c, k_c, v_c, g_c, gm_c = _in_proj(ctx, mc[0], mc[1], tm=tm_c, **rope_c, **wts)

        logit_bound = (HEAD_DIM * Q_SCALE) * jnp.max(jnp.abs(q_norm_g[l])) * jnp.max(jnp.abs(k_norm_g[l]))
        a_x = _attention(q_x, (k_c, k_x), (v_c, v_x), logit_bound, tq=KEY_CHUNK)
        f_x = _fourier(g_x.reshape(b, 2 * s, fw))

        wo = w_out[l].astype(BF16)
        w1 = w_mlp1[l].astype(BF16)
        w2 = w_mlp2[l].astype(BF16)
        g2 = norm2_g[l].reshape(1, d)
        x = _out_mlp(x, a_x, f_x, gm_x, wo, g2, mx[2], mx[3], mx[4], mx[5], w1, w2, tm=tm_x)

        if l < depth - 1:
            a_c = _attention(q_c, (k_c,), (v_c,), logit_bound, tq=KEY_CHUNK)
            f_c = _fourier(g_c.reshape(b, 2 * n_ctx, fw))
            ctx = _out_mlp(ctx, a_c, f_c, gm_c, wo, g2, mc[2], mc[3], mc[4], mc[5], w1, w2, tm=tm_c)
    return x
```

```python
import functools

import numpy as np
import jax
import jax.numpy as jnp
from jax import lax
from jax.experimental import pallas as pl
from jax.experimental.pallas import tpu as pltpu

GRID_W = 64
HEAD_DIM = 64
AXIS_DIM = HEAD_DIM // 2
ROT_HALF = AXIS_DIM // 2
GQA_GROUP = 4
FOURIER_GROUP = 64
GMLP_HEAD = 64
CHUNK = 128
ROPE_THETA = 10000.0
EPS = 1e-6

VMEM_LIMIT_BYTES = 56 * 1024 * 1024
KEY_CHUNK = 256
DFT_RADIX = 64
LOG2_E = 1.4426950408889634
Q_SCALE = HEAD_DIM ** -0.5 * LOG2_E
MAX_UNSHIFTED_LOGIT = 60.0

BF16 = jnp.bfloat16
F32 = jnp.float32


def _params(*sem):
    return pltpu.CompilerParams(dimension_semantics=sem, vmem_limit_bytes=VMEM_LIMIT_BYTES)


def _dot(a, b):
    return jnp.dot(a, b, preferred_element_type=F32)


def _dot_nt(a, b):
    return lax.dot_general(a, b, (((1,), (1,)), ((), ())), preferred_element_type=F32)


def _rms_rows(x):
    return x * lax.rsqrt(jnp.mean(x * x, axis=-1, keepdims=True) + EPS)


def _adaln_kernel(c_ref, w_ref, b_ref, o_ref):
    h = jax.nn.silu(c_ref[...]).astype(BF16)
    o_ref[...] = _dot(h, w_ref[0].astype(BF16)) + b_ref[0]


def _adaln(cond, w, b, layer):
    rows, d = cond.shape
    depth, _, n = w.shape
    tn = 1536
    return pl.pallas_call(
        _adaln_kernel,
        grid=(n // tn,),
        in_specs=[pl.BlockSpec((rows, d), lambda j: (0, 0)),
                  pl.BlockSpec((1, d, tn), lambda j: (layer, 0, j)),
                  pl.BlockSpec((1, 1, tn), lambda j: (layer, 0, j))],
        out_specs=pl.BlockSpec((rows, tn), lambda j: (0, j)),
        out_shape=jax.ShapeDtypeStruct((rows, n), F32),
        compiler_params=_params("arbitrary"),
        name="adaln",
    )(cond, w, b.reshape(depth, 1, n))


def _gelu_tanh(x):
    c = float(np.sqrt(2.0 / np.pi))
    hx = 0.5 * x
    return hx + hx * jnp.tanh(x * (c + (c * 0.044715) * (x * x)))


def _head_norm_rope_t(p, rope):
    r = lax.rsqrt(jnp.mean(p * p, axis=0, keepdims=True) + EPS)
    h = ROT_HALF
    psw = jnp.concatenate([p[h:2 * h], p[0:h], p[3 * h:4 * h], p[2 * h:3 * h]], axis=0)
    return (p * rope[0] + psw * rope[1]) * r


def _in_proj_kernel(x_ref, sh_ref, sc_ref, g1_ref, wt_ref, wn_ref, rq_ref, rk_ref,
                    cs_ref, vg_ref, ws_ref, bs_ref,
                    q_ref, k_ref, v_ref, g_ref, gm_ref, *, n_q_heads, n_kv_heads):
    tm = x_ref.shape[1]
    sub = KEY_CHUNK
    d = HEAD_DIM
    fw = cs_ref.shape[0]
    gw = vg_ref.shape[1]
    n_heads = gw // GMLP_HEAD
    gain = g1_ref[...] * (1.0 + sc_ref[0, 0])
    shift = sh_ref[0, 0]
    head_of_lane = lax.broadcasted_iota(jnp.int32, (1, gw), 1) // GMLP_HEAD

    def project(j):
        h = (_rms_rows(x_ref[0, j * sub:(j + 1) * sub, :]) * gain + shift).astype(BF16)
        return _dot_nt(wt_ref[...], h), _dot(h, wn_ref[...])

    def finish(j, pt, pn):
        tok = slice(j * sub, (j + 1) * sub)
        rq = rq_ref[:, :, tok]
        rk = rk_ref[:, :, tok]
        for i in range(n_q_heads):
            q_ref[0, i * d:(i + 1) * d, tok] = _head_norm_rope_t(pt[i * d:(i + 1) * d], rq).astype(BF16)
        ko = n_q_heads * d
        for i in range(n_kv_heads):
            kh = _head_norm_rope_t(pt[ko + i * d:ko + (i + 1) * d], rk)
            k_ref[0, i, tok, :] = jnp.concatenate([kh, jnp.zeros_like(kh)], axis=0).T.astype(BF16)
        vo = ko + n_kv_heads * d
        v_ref[0, j] = pt[vo:vo + n_kv_heads * d].astype(BF16)

        fcs = _dot(pn[:, :fw].astype(BF16), cs_ref[...])
        g_ref[0, 0, tok, :] = fcs[:, :fw].astype(BF16)
        g_ref[0, 1, tok, :] = fcs[:, fw:].astype(BF16)

        u = _gelu_tanh(pn[:, fw:fw + gw])
        gv = _gelu_tanh(pn[:, fw + gw:fw + 2 * gw])
        sq = gv * gv
        rinv = jnp.zeros_like(gv)
        for i in range(n_heads):
            m = head_of_lane == i
            ms = jnp.sum(jnp.where(m, sq, 0.0), axis=-1, keepdims=True) * (1.0 / GMLP_HEAD)
            rinv = jnp.where(m, lax.rsqrt(ms + EPS), rinv)
        vn = (gv * rinv * vg_ref[...]).astype(BF16)
        bs = bs_ref[...]
        for c in range(sub // CHUNK):
            vc = vn[c * CHUNK:(c + 1) * CHUNK]
            sp = bs
            for i in range(n_heads):
                sp = sp + jnp.where(head_of_lane == i, _dot(ws_ref[i], vc), 0.0)
            rows = slice(j * sub + c * CHUNK, j * sub + (c + 1) * CHUNK)
            gm_ref[0, rows, :] = (u[c * CHUNK:(c + 1) * CHUNK] * sp).astype(BF16)

    nxt = project(0)
    for j in range(tm // sub):
        cur = nxt
        if j + 1 < tm // sub:
            nxt = project(j + 1)
        finish(j, *cur)


def _mod_spec(mod, row_of, term):
    return pl.BlockSpec((1, 1, 1, mod.shape[3]), lambda bi, i: (row_of(bi), term, 0, 0))


def _in_proj(x, mod, row_of, g1, wt, wn, rq, rk, cs, vg, ws, bs, *, tm):
    b, n, d = x.shape
    n_kv = 2
    n_q = n_kv * GQA_GROUP
    qw = n_q * HEAD_DIM
    kw = n_kv * HEAD_DIM
    fw = cs.shape[0]
    gw = vg.shape[1]
    cpt = tm // KEY_CHUNK
    const2 = lambda bi, i: (0, 0)
    const3 = lambda bi, i: (0, 0, 0)
    kern = functools.partial(_in_proj_kernel, n_q_heads=n_q, n_kv_heads=n_kv)
    return pl.pallas_call(
        kern,
        grid=(b, n // tm),
        in_specs=[pl.BlockSpec((1, tm, d), lambda bi, i: (bi, i, 0)),
                  _mod_spec(mod, row_of, 0),
                  _mod_spec(mod, row_of, 1),
                  pl.BlockSpec((1, d), const2),
                  pl.BlockSpec(wt.shape, const2),
                  pl.BlockSpec(wn.shape, const2),
                  pl.BlockSpec((2, HEAD_DIM, tm), lambda bi, i: (0, 0, i)),
                  pl.BlockSpec((2, HEAD_DIM, tm), lambda bi, i: (0, 0, i)),
                  pl.BlockSpec(cs.shape, const2),
                  pl.BlockSpec((1, gw), const2),
                  pl.BlockSpec(ws.shape, const3),
                  pl.BlockSpec(bs.shape, const2)],
        out_specs=[pl.BlockSpec((1, qw, tm), lambda bi, i: (bi, 0, i)),
                   pl.BlockSpec((1, n_kv, tm, 2 * HEAD_DIM), lambda bi, i: (bi, 0, i, 0)),
                   pl.BlockSpec((1, cpt, kw, KEY_CHUNK), lambda bi, i: (bi, i, 0, 0)),
                   pl.BlockSpec((1, 2, tm, fw), lambda bi, i: (bi, 0, i, 0)),
                   pl.BlockSpec((1, tm, gw), lambda bi, i: (bi, i, 0))],
        out_shape=[jax.ShapeDtypeStruct((b, qw, n), BF16),
                   jax.ShapeDtypeStruct((b, n_kv, n, 2 * HEAD_DIM), BF16),
                   jax.ShapeDtypeStruct((b, n // KEY_CHUNK, kw, KEY_CHUNK), BF16),
                   jax.ShapeDtypeStruct((b, 2, n, fw), BF16),
                   jax.ShapeDtypeStruct((b, n, gw), BF16)],
        compiler_params=_params("arbitrary", "arbitrary"),
        name="in_proj",
    )(x, mod, mod, g1, wt, wn, rq, rk, cs, vg, ws, bs)


def _attn_queries(q_ref, cols):
    d = HEAD_DIM
    qcat = jnp.concatenate([q_ref[0, g * d:(g + 1) * d, cols] for g in range(GQA_GROUP)], axis=1)
    return jnp.concatenate([qcat, jnp.zeros_like(qcat)], axis=0)


def _attn_finish(acc, o_ref, rows):
    d = HEAD_DIM
    tq = acc.shape[1] // GQA_GROUP
    out = acc[:d] * (1.0 / acc[d:d + 1])
    for j in range(GQA_GROUP // 2):
        pair = jnp.concatenate([out[:, (2 * j) * tq:(2 * j + 1) * tq],
                                out[:, (2 * j + 1) * tq:(2 * j + 2) * tq]], axis=0)
        o_ref[0, rows, 2 * j * d:(2 * j + 2) * d] = pair.T.astype(BF16)


def _attn_sources(refs):
    n_src = (len(refs) - 1) // 2
    return refs[:n_src], refs[n_src:2 * n_src], refs[2 * n_src]


def _attn_kernel_bounded(q_ref, *refs, tq):
    k_refs, v_refs, o_ref = _attn_sources(refs)
    chunks = [(j, c) for j, kr in enumerate(k_refs) for c in range(kr.shape[2] // KEY_CHUNK)]
    n_sub = q_ref.shape[2] // tq
    sub = lambda i: slice(i * tq, (i + 1) * tq)
    heads = range(GQA_GROUP)
    qcats = [_attn_queries(q_ref, sub(i)) for i in range(n_sub)]
    work = [(i, jc) for i in range(n_sub) for jc in chunks]
    scores = lambda i, jc, g: _dot(k_refs[jc[0]][0, 0, jc[1] * KEY_CHUNK:(jc[1] + 1) * KEY_CHUNK, :],
                                   qcats[i][:, g * tq:(g + 1) * tq])
    s_next = [scores(*work[0], g) for g in heads]
    acc = den = None
    for w, (i, jc) in enumerate(work):
        s = s_next
        pf = [jnp.exp2(s[g]) for g in heads]
        ps = [jnp.sum(pf[g].reshape(KEY_CHUNK // 8, 8, tq), axis=0) for g in heads]
        s_next, pv = [], []
        for g in heads:
            if w + 1 < len(work):
                s_next.append(scores(*work[w + 1], g))
            pv.append(_dot(v_refs[jc[0]][0, jc[1]], pf[g].astype(BF16)))
        acc = pv if jc == chunks[0] else [acc[g] + pv[g] for g in heads]
        den = ps if jc == chunks[0] else [den[g] + ps[g] for g in heads]
        if jc == chunks[-1]:
            total = [jnp.broadcast_to(jnp.sum(den[g], axis=0, keepdims=True), den[g].shape) for g in heads]
            full = jnp.concatenate([jnp.concatenate(acc, axis=1), jnp.concatenate(total, axis=1)], axis=0)
            _attn_finish(full, o_ref, sub(i))


def _attn_kernel_online(q_ref, *refs, tq):
    k_refs, v_refs, o_ref = _attn_sources(refs)
    assert q_ref.shape[2] == tq
    qcat = _attn_queries(q_ref, slice(0, tq))
    ones = jnp.ones((16, KEY_CHUNK), BF16)
    carry = (jnp.full((1, qcat.shape[1]), -1e30, F32), jnp.zeros((HEAD_DIM + 16, qcat.shape[1]), F32))
    for k_ref, v_ref in zip(k_refs, v_refs):
        def step(c, carry, k_ref=k_ref, v_ref=v_ref):
            m_old, acc = carry
            start = pl.multiple_of(c * KEY_CHUNK, KEY_CHUNK)
            s = _dot(k_ref[0, 0, pl.ds(start, KEY_CHUNK), :], qcat)
            m_new = jnp.maximum(m_old, jnp.max(s, axis=0, keepdims=True))
            alpha = jnp.exp2(m_old - m_new)
            p = jnp.exp2((s - m_new).astype(BF16))
            va = jnp.concatenate([v_ref[0, c], ones], axis=0)
            return m_new, acc * alpha + _dot(va, p)

        carry = lax.fori_loop(0, k_ref.shape[2] // KEY_CHUNK, step, carry)
    _attn_finish(carry[1], o_ref, slice(0, tq))


def _attention(qt, ks, vs, logit_bound, *, tq):
    n_sub = max(m for m in (1, 2, 4) if qt.shape[2] % (m * tq) == 0)
    bounded = functools.partial(_attention_call, body=_attn_kernel_bounded, tq=tq, n_sub=n_sub)
    online = functools.partial(_attention_call, body=_attn_kernel_online, tq=tq, n_sub=1)
    return lax.cond(logit_bound <= MAX_UNSHIFTED_LOGIT, bounded, online, qt, *ks, *vs)


def _attention_call(qt, *kv, body, tq, n_sub):
    b, qw, s = qt.shape
    ks, vs = kv[:len(kv) // 2], kv[len(kv) // 2:]
    gq = GQA_GROUP * HEAD_DIM
    tb = n_sub * tq
    k_spec = lambda k: pl.BlockSpec((1, 1) + k.shape[2:], lambda bi, kv, i: (bi, kv, 0, 0))
    v_spec = lambda v: pl.BlockSpec((1, v.shape[1], HEAD_DIM, KEY_CHUNK), lambda bi, kv, i: (bi, 0, kv, 0))
    return pl.pallas_call(
        functools.partial(body, tq=tq),
        grid=(b, ks[0].shape[1], s // tb),
        in_specs=[pl.BlockSpec((1, gq, tb), lambda bi, kv, i: (bi, kv, i))]
                 + [k_spec(k) for k in ks] + [v_spec(v) for v in vs],
        out_specs=pl.BlockSpec((1, tb, gq), lambda bi, kv, i: (bi, i, kv)),
        out_shape=jax.ShapeDtypeStruct((b, s, qw), BF16),
        compiler_params=_params("arbitrary", "arbitrary", "arbitrary"),
        name="attention",
    )(qt, *ks, *vs)


def _fourier_kernel(chi_ref, shi_ref, clo_ref, slo_ref, mirror_ref, g_ref, lo_ref, hi_ref, wc_scr, ws_scr,
                    *, scale, hi_per_tile):
    n = clo_ref.shape[1]
    tm = hi_per_tile * DFT_RADIX
    i = pl.program_id(0)

    @pl.when(pl.program_id(1) == 0)
    def _():
        clo = clo_ref[...]
        slo = slo_ref[...]
        for j in range(hi_per_tile):
            chi = chi_ref[pl.ds(i * hi_per_tile + j, 1), :]
            shi = shi_ref[pl.ds(i * hi_per_tile + j, 1), :]
            rows = slice(j * DFT_RADIX, (j + 1) * DFT_RADIX)
            wc_scr[rows, :] = (chi * clo - shi * slo).astype(BF16)
            ws_scr[rows, :] = (shi * clo + chi * slo).astype(BF16)
        nxt = (i + 1) * hi_per_tile
        pad = wc_scr.shape[0] - tm
        wc_scr[tm:, :] = jnp.broadcast_to(chi_ref[pl.ds(nxt, 1), :], (pad, n)).astype(BF16)
        ws_scr[tm:, :] = jnp.broadcast_to(shi_ref[pl.ds(nxt, 1), :], (pad, n)).astype(BF16)

    yc = _dot(wc_scr[...], g_ref[0, 0:n, :])
    ys = _dot(ws_scr[...], g_ref[0, n:2 * n, :])
    lo_ref[0] = ((yc[:tm] - ys[:tm]) * scale).astype(BF16)
    hi_ref[0] = _dot(mirror_ref[...], ((yc + ys) * scale).astype(BF16)).astype(BF16)


def _dft_tables(n):
    hi = n // DFT_RADIX
    t = np.arange(n, dtype=np.int64)
    a_hi = (DFT_RADIX * np.arange(hi, dtype=np.int64)[:, None] * t[None, :]) % n
    a_lo = (np.arange(DFT_RADIX, dtype=np.int64)[:, None] * t[None, :]) % n
    f = lambda fn, a: jnp.asarray(fn(2.0 * np.pi * a / n), dtype=F32)
    return f(np.cos, a_hi), f(np.sin, a_hi), f(np.cos, a_lo), f(np.sin, a_lo)


def _fourier(g):
    b, n2, fw = g.shape
    n = n2 // 2
    chi, shi, clo, slo = _dft_tables(n)
    n_hi = n // DFT_RADIX
    hi_per_tile = min(8, n_hi // 2)
    tmf = hi_per_tile * DFT_RADIX
    n_tiles = n // (2 * tmf)
    rows_ext = tmf + 16
    mirror = np.zeros((tmf, rows_ext), np.float32)
    mirror[np.arange(tmf), tmf - np.arange(tmf)] = 1.0
    mirror = jnp.asarray(mirror).astype(BF16)
    const = lambda a: pl.BlockSpec(a.shape, lambda i, bi: (0, 0))
    kern = functools.partial(_fourier_kernel, scale=float(1.0 / np.sqrt(n * FOURIER_GROUP)),
                             hi_per_tile=hi_per_tile)
    half = jax.ShapeDtypeStruct((b, n // 2, fw), BF16)
    lo, hi = pl.pallas_call(
        kern,
        grid=(n_tiles, b),
        in_specs=[const(chi), const(shi), const(clo), const(slo), const(mirror),
                  pl.BlockSpec((1, n2, fw), lambda i, bi: (bi, 0, 0))],
        out_specs=[pl.BlockSpec((1, tmf, fw), lambda i, bi: (bi, i, 0)),
                   pl.BlockSpec((1, tmf, fw), lambda i, bi: (bi, n_tiles - 1 - i, 0))],
        out_shape=[half, half],
        scratch_shapes=[pltpu.VMEM((rows_ext, n), BF16), pltpu.VMEM((rows_ext, n), BF16)],
        compiler_params=_params("arbitrary", "arbitrary"),
        name="fourier",
    )(chi, shi, clo, slo, mirror, g)
    return lo, hi


def _out_mlp_kernel(x_ref, a_ref, flo_ref, fhi_ref, m_ref, wo_ref, g2_ref, gt1_ref, sh_ref, sc_ref, gt2_ref,
                    w1_ref, w2_ref, o_ref, *, ff_chunk, lo_tiles):
    aw = a_ref.shape[2]
    fw = flo_ref.shape[2]
    if lo_tiles:
        four = jnp.where(pl.program_id(1) < lo_tiles, flo_ref[0], fhi_ref[0])
    else:
        four = jnp.concatenate([flo_ref[0], fhi_ref[0]], axis=0)
    gain2 = g2_ref[...] * (1.0 + sc_ref[0, 0])
    tm = x_ref.shape[1]
    sub = min(tm, KEY_CHUNK)

    def out_proj(rows):
        mix = (_dot(a_ref[0, rows, :], wo_ref[0:aw]) + _dot(four[rows], wo_ref[aw:aw + fw])
               + _dot(m_ref[0, rows, :], wo_ref[aw + fw:]))
        return x_ref[0, rows, :] + gt1_ref[0, 0] * mix

    def mlp(rows, x1):
        h = (_rms_rows(x1) * gain2 + sh_ref[0, 0]).astype(BF16)
        y = jnp.zeros_like(x1)
        for j in range(w1_ref.shape[1] // ff_chunk):
            cols = slice(j * ff_chunk, (j + 1) * ff_chunk)
            a = jnp.square(jnp.maximum(_dot(h, w1_ref[:, cols]), 0.0)).astype(BF16)
            y = y + _dot(a, w2_ref[cols, :])
        o_ref[0, rows, :] = x1 + gt2_ref[0, 0] * y

    tiles = [slice(j * sub, (j + 1) * sub) for j in range(tm // sub)]
    x1s = [out_proj(rows) for rows in tiles]
    for rows, x1 in zip(tiles, x1s):
        mlp(rows, x1)


def _out_mlp(x, attn, four, gm, wo, g2, mod, row_of, w1, w2, *, tm):
    b, n, d = x.shape
    f_lo, f_hi = four
    half = f_lo.shape[1]
    fw = f_lo.shape[2]
    tok = lambda w: pl.BlockSpec((1, tm, w), lambda bi, i: (bi, i, 0))
    if tm == n:
        lo_tiles = 0
        lo_spec = hi_spec = pl.BlockSpec((1, half, fw), lambda bi, i: (bi, 0, 0))
    else:
        lo_tiles = half // tm
        lo_spec = pl.BlockSpec((1, tm, fw), lambda bi, i: (bi, jnp.minimum(i, lo_tiles - 1), 0))
        hi_spec = pl.BlockSpec((1, tm, fw), lambda bi, i: (bi, jnp.maximum(i - lo_tiles, 0), 0))
    const = lambda a: pl.BlockSpec(a.shape, lambda bi, i: (0, 0), pipeline_mode=pl.Buffered(1))
    kern = functools.partial(_out_mlp_kernel, ff_chunk=1024, lo_tiles=lo_tiles)
    return pl.pallas_call(
        kern,
        grid=(b, n // tm),
        in_specs=[tok(d), tok(attn.shape[2]), lo_spec, hi_spec, tok(gm.shape[2]),
                  const(wo), pl.BlockSpec((1, d), lambda bi, i: (0, 0)),
                  *[_mod_spec(mod, row_of, term) for term in (2, 3, 4, 5)], const(w1), const(w2)],
        out_specs=tok(d),
        out_shape=jax.ShapeDtypeStruct((b, n, d), F32),
        compiler_params=_params("arbitrary", "arbitrary"),
        name="out_mlp",
    )(x, attn, f_lo, f_hi, gm, wo, g2, mod, mod, mod, mod, w1, w2)


def _rope_tables_t(n):
    rows = n // GRID_W
    row = jnp.repeat(jnp.arange(rows, dtype=F32), GRID_W)
    col = jnp.tile(jnp.arange(GRID_W, dtype=F32), rows)
    inv = ROPE_THETA ** (-jnp.arange(0, AXIS_DIM, 2, dtype=F32) / AXIS_DIM)
    ar = (row[:, None] * inv).T
    ac = (col[:, None] * inv).T
    ct = jnp.concatenate([jnp.cos(ar), jnp.cos(ar), jnp.cos(ac), jnp.cos(ac)], axis=0)
    st = jnp.concatenate([-jnp.sin(ar), jnp.sin(ar), -jnp.sin(ac), jnp.sin(ac)], axis=0)
    return ct, st


def _gained_rope(ct, st, gain, scale):
    h = ROT_HALF
    gain_sw = jnp.concatenate([gain[h:2 * h], gain[0:h], gain[3 * h:4 * h], gain[2 * h:3 * h]])
    return jnp.stack([ct * (gain * scale)[:, None], st * (gain_sw * scale)[:, None]])


def _channel_dft():
    c = np.arange(FOURIER_GROUP)
    ang = 2.0 * np.pi * ((c[:, None] * c[None, :]) % FOURIER_GROUP) / FOURIER_GROUP
    return np.cos(ang), np.sin(ang)


def kernel(x, c, ctx, c_ctx, w_ada, b_ada, norm1_g, norm2_g, w_in, q_norm_g, k_norm_g, gmlp_v_g,
           w_spatial, b_spatial, w_out, w_mlp1, w_mlp2):
    depth = w_ada.shape[0]
    b, s, d = x.shape
    n_ctx = ctx.shape[1]
    gw = gmlp_v_g.shape[1]
    n_gmlp_heads = gw // GMLP_HEAD
    fw = d // 4
    qkv_w = w_in.shape[2] - fw - 2 * gw

    cond = jnp.zeros((16, d), F32).at[:b].set(c).at[b].set(c_ctx)
    ct_x, st_x = _rope_tables_t(s)
    ct_c = jnp.ones((HEAD_DIM, n_ctx), F32)
    st_c = jnp.zeros((HEAD_DIM, n_ctx), F32)
    cc, sc = _channel_dft()
    eye = np.eye(fw // FOURIER_GROUP)
    cs = jnp.asarray(np.concatenate([np.kron(eye, cc), np.kron(eye, sc)], axis=1), dtype=F32).astype(BF16)

    tm_x = 512 if s % 512 == 0 else KEY_CHUNK
    tm_in = 1024 if s % 1024 == 0 else tm_x
    tm_c = KEY_CHUNK

    for l in range(depth):
        mod = _adaln(cond, w_ada, b_ada, l).reshape(cond.shape[0], 6, 1, d)
        x_row = lambda bi: bi
        c_row = lambda bi: b

        wt = w_in[l][:, :qkv_w].T.astype(BF16)
        wn = w_in[l][:, qkv_w:].astype(BF16)
        wts = dict(g1=norm1_g[l].reshape(1, d), wt=wt, wn=wn,
                   cs=cs, vg=gmlp_v_g[l].reshape(1, gw), ws=w_spatial[l].astype(BF16),
                   bs=jnp.repeat(b_spatial[l].T, GMLP_HEAD, axis=1))
        rope_x = dict(rq=_gained_rope(ct_x, st_x, q_norm_g[l], Q_SCALE), rk=_gained_rope(ct_x, st_x, k_norm_g[l], 1.0))
        rope_c = dict(rq=_gained_rope(ct_c, st_c, q_norm_g[l], Q_SCALE), rk=_gained_rope(ct_c, st_c, k_norm_g[l], 1.0))
        q_x, k_x, v_x, g_x, gm_x = _in_proj(x, mod, x_row, tm=tm_in, **rope_x, **wts)
        q_c, k_c, v_c, g_c, gm_c = _in_proj(ctx, mod, c_row, tm=tm_c, **rope_c, **wts)

        logit_bound = (HEAD_DIM * Q_SCALE) * jnp.max(jnp.abs(q_norm_g[l])) * jnp.max(jnp.abs(k_norm_g[l]))
        a_x = _attention(q_x, (k_c, k_x), (v_c, v_x), logit_bound, tq=KEY_CHUNK)
        f_x = _fourier(g_x.reshape(b, 2 * s, fw))

        wo = w_out[l].astype(BF16)
        w1 = w_mlp1[l].astype(BF16)
        w2 = w_mlp2[l].astype(BF16)
        g2 = norm2_g[l].reshape(1, d)
        x = _out_mlp(x, a_x, f_x, gm_x, wo, g2, mod, x_row, w1, w2, tm=tm_x)

        if l < depth - 1:
            a_c = _attention(q_c, (k_c,), (v_c,), logit_bound, tq=KEY_CHUNK)
            f_c = _fourier(g_c.reshape(b, 2 * n_ctx, fw))
            ctx = _out_mlp(ctx, a_c, f_c, gm_c, wo, g2, mod, c_row, w1, w2, tm=tm_c)
    return x
```

```python
import functools

import numpy as np
import jax
import jax.numpy as jnp
from jax import lax
from jax.experimental import pallas as pl
from jax.experimental.pallas import tpu as pltpu

GRID_W = 64
HEAD_DIM = 64
AXIS_DIM = HEAD_DIM // 2
ROT_HALF = AXIS_DIM // 2
GQA_GROUP = 4
FOURIER_GROUP = 64
GMLP_HEAD = 64
CHUNK = 128
ROPE_THETA = 10000.0
EPS = 1e-6

VMEM_LIMIT_BYTES = 56 * 1024 * 1024
KEY_CHUNK = 256
DFT_RADIX = 64
LOG2_E = 1.4426950408889634
Q_SCALE = HEAD_DIM ** -0.5 * LOG2_E
MAX_UNSHIFTED_LOGIT = 60.0

BF16 = jnp.bfloat16
F32 = jnp.float32


def _params(*sem):
    return pltpu.CompilerParams(dimension_semantics=sem, vmem_limit_bytes=VMEM_LIMIT_BYTES)


def _dot(a, b):
    return jnp.dot(a, b, preferred_element_type=F32)


def _dot_nt(a, b):
    return lax.dot_general(a, b, (((1,), (1,)), ((), ())), preferred_element_type=F32)


def _rms_rows(x):
    return x * lax.rsqrt(jnp.mean(x * x, axis=-1, keepdims=True) + EPS)


def _adaln_kernel(c_ref, w_ref, b_ref, o_ref):
    h = jax.nn.silu(c_ref[...]).astype(BF16)
    o_ref[...] = _dot(h, w_ref[0].astype(BF16)) + b_ref[0]


def _adaln(cond, w, b, layer):
    rows, d = cond.shape
    depth, _, n = w.shape
    tn = 1536
    return pl.pallas_call(
        _adaln_kernel,
        grid=(n // tn,),
        in_specs=[pl.BlockSpec((rows, d), lambda j: (0, 0)),
                  pl.BlockSpec((1, d, tn), lambda j: (layer, 0, j)),
                  pl.BlockSpec((1, 1, tn), lambda j: (layer, 0, j))],
        out_specs=pl.BlockSpec((rows, tn), lambda j: (0, j)),
        out_shape=jax.ShapeDtypeStruct((rows, n), F32),
        compiler_params=_params("arbitrary"),
        name="adaln",
    )(cond, w, b.reshape(depth, 1, n))


def _gelu_tanh(x):
    c = float(np.sqrt(2.0 / np.pi))
    hx = 0.5 * x
    return hx + hx * jnp.tanh(x * (c + (c * 0.044715) * (x * x)))


def _head_norm_rope_t(p, rope):
    r = lax.rsqrt(jnp.mean(p * p, axis=0, keepdims=True) + EPS)
    h = ROT_HALF
    psw = jnp.concatenate([p[h:2 * h], p[0:h], p[3 * h:4 * h], p[2 * h:3 * h]], axis=0)
    return (p * rope[0] + psw * rope[1]) * r


def _in_proj_kernel(x_ref, sh_ref, sc_ref, g1_ref, wt_ref, wn_ref, rq_ref, rk_ref,
                    cs_ref, vg_ref, ws_ref, bs_ref,
                    q_ref, k_ref, v_ref, g_ref, gm_ref, *, n_q_heads, n_kv_heads):
    tm = x_ref.shape[1]
    sub = KEY_CHUNK
    d = HEAD_DIM
    fw = cs_ref.shape[0]
    gw = vg_ref.shape[1]
    n_heads = gw // GMLP_HEAD
    gain = g1_ref[...] * (1.0 + sc_ref[0, 0])
    shift = sh_ref[0, 0]
    head_of_lane = lax.broadcasted_iota(jnp.int32, (1, gw), 1) // GMLP_HEAD

    def project(j):
        h = (_rms_rows(x_ref[0, j * sub:(j + 1) * sub, :]) * gain + shift).astype(BF16)
        return _dot_nt(wt_ref[...], h), _dot(h, wn_ref[...])

    def finish(j, pt, pn):
        tok = slice(j * sub, (j + 1) * sub)
        rq = rq_ref[:, :, tok]
        rk = rk_ref[:, :, tok]
        for i in range(n_q_heads):
            q_ref[0, i * d:(i + 1) * d, tok] = _head_norm_rope_t(pt[i * d:(i + 1) * d], rq).astype(BF16)
        ko = n_q_heads * d
        for i in range(n_kv_heads):
            kh = _head_norm_rope_t(pt[ko + i * d:ko + (i + 1) * d], rk)
            k_ref[0, i, tok, :] = jnp.concatenate([kh, jnp.zeros_like(kh)], axis=0).T.astype(BF16)
        vo = ko + n_kv_heads * d
        v_ref[0, j] = pt[vo:vo + n_kv_heads * d].astype(BF16)

        fcs = _dot(pn[:, :fw].astype(BF16), cs_ref[...])
        g_ref[0, 0, tok, :] = fcs[:, :fw].astype(BF16)
        g_ref[0, 1, tok, :] = fcs[:, fw:].astype(BF16)

        u = _gelu_tanh(pn[:, fw:fw + gw])
        gv = _gelu_tanh(pn[:, fw + gw:fw + 2 * gw])
        sq = gv * gv
        rinv = jnp.zeros_like(gv)
        for i in range(n_heads):
            m = head_of_lane == i
            ms = jnp.sum(jnp.where(m, sq, 0.0), axis=-1, keepdims=True) * (1.0 / GMLP_HEAD)
            rinv = jnp.where(m, lax.rsqrt(ms + EPS), rinv)
        vn = (gv * rinv * vg_ref[...]).astype(BF16)
        bs = bs_ref[...]
        for c in range(sub // CHUNK):
            vc = vn[c * CHUNK:(c + 1) * CHUNK]
            vblk = jnp.concatenate([jnp.where(head_of_lane == i, vc, jnp.zeros_like(vc))
                                    for i in range(n_heads)], axis=0)
            sp = bs + _dot(ws_ref[...], vblk)
            rows = slice(j * sub + c * CHUNK, j * sub + (c + 1) * CHUNK)
            gm_ref[0, rows, :] = (u[c * CHUNK:(c + 1) * CHUNK] * sp).astype(BF16)

    nxt = project(0)
    for j in range(tm // sub):
        cur = nxt
        if j + 1 < tm // sub:
            nxt = project(j + 1)
        finish(j, *cur)


def _mod_spec(mod, row_of, term):
    return pl.BlockSpec((1, 1, 1, mod.shape[3]), lambda bi, i: (row_of(bi), term, 0, 0))


def _in_proj(x, mod, row_of, g1, wt, wn, rq, rk, cs, vg, ws, bs, *, tm):
    b, n, d = x.shape
    n_kv = 2
    n_q = n_kv * GQA_GROUP
    qw = n_q * HEAD_DIM
    kw = n_kv * HEAD_DIM
    fw = cs.shape[0]
    gw = vg.shape[1]
    cpt = tm // KEY_CHUNK
    const2 = lambda bi, i: (0, 0)
    kern = functools.partial(_in_proj_kernel, n_q_heads=n_q, n_kv_heads=n_kv)
    return pl.pallas_call(
        kern,
        grid=(b, n // tm),
        in_specs=[pl.BlockSpec((1, tm, d), lambda bi, i: (bi, i, 0)),
                  _mod_spec(mod, row_of, 0),
                  _mod_spec(mod, row_of, 1),
                  pl.BlockSpec((1, d), const2),
                  pl.BlockSpec(wt.shape, const2),
                  pl.BlockSpec(wn.shape, const2),
                  pl.BlockSpec((2, HEAD_DIM, tm), lambda bi, i: (0, 0, i)),
                  pl.BlockSpec((2, HEAD_DIM, tm), lambda bi, i: (0, 0, i)),
                  pl.BlockSpec(cs.shape, const2),
                  pl.BlockSpec((1, gw), const2),
                  pl.BlockSpec(ws.shape, const2),
                  pl.BlockSpec(bs.shape, const2)],
        out_specs=[pl.BlockSpec((1, qw, tm), lambda bi, i: (bi, 0, i)),
                   pl.BlockSpec((1, n_kv, tm, 2 * HEAD_DIM), lambda bi, i: (bi, 0, i, 0)),
                   pl.BlockSpec((1, cpt, kw, KEY_CHUNK), lambda bi, i: (bi, i, 0, 0)),
                   pl.BlockSpec((1, 2, tm, fw), lambda bi, i: (bi, 0, i, 0)),
                   pl.BlockSpec((1, tm, gw), lambda bi, i: (bi, i, 0))],
        out_shape=[jax.ShapeDtypeStruct((b, qw, n), BF16),
                   jax.ShapeDtypeStruct((b, n_kv, n, 2 * HEAD_DIM), BF16),
                   jax.ShapeDtypeStruct((b, n // KEY_CHUNK, kw, KEY_CHUNK), BF16),
                   jax.ShapeDtypeStruct((b, 2, n, fw), BF16),
                   jax.ShapeDtypeStruct((b, n, gw), BF16)],
        compiler_params=_params("arbitrary", "arbitrary"),
        name="in_proj",
    )(x, mod, mod, g1, wt, wn, rq, rk, cs, vg, ws, bs)


def _attn_queries(q_ref, cols):
    d = HEAD_DIM
    qcat = jnp.concatenate([q_ref[0, g * d:(g + 1) * d, cols] for g in range(GQA_GROUP)], axis=1)
    return jnp.concatenate([qcat, jnp.zeros_like(qcat)], axis=0)


def _attn_finish(acc, o_ref, rows):
    d = HEAD_DIM
    tq = acc.shape[1] // GQA_GROUP
    out = acc[:d] * (1.0 / acc[d:d + 1])
    for j in range(GQA_GROUP // 2):
        pair = jnp.concatenate([out[:, (2 * j) * tq:(2 * j + 1) * tq],
                                out[:, (2 * j + 1) * tq:(2 * j + 2) * tq]], axis=0)
        o_ref[0, rows, 2 * j * d:(2 * j + 2) * d] = pair.T.astype(BF16)


def _attn_sources(refs):
    n_src = (len(refs) - 1) // 2
    return refs[:n_src], refs[n_src:2 * n_src], refs[2 * n_src]


def _attn_kernel_bounded(q_ref, *refs, tq):
    k_refs, v_refs, o_ref = _attn_sources(refs)
    chunks = [(j, c) for j, kr in enumerate(k_refs) for c in range(kr.shape[2] // KEY_CHUNK)]
    n_sub = q_ref.shape[2] // tq
    sub = lambda i: slice(i * tq, (i + 1) * tq)
    heads = range(GQA_GROUP)
    qcats = [_attn_queries(q_ref, sub(i)) for i in range(n_sub)]
    work = [(i, jc) for i in range(n_sub) for jc in chunks]
    scores = lambda i, jc, g: _dot(k_refs[jc[0]][0, 0, jc[1] * KEY_CHUNK:(jc[1] + 1) * KEY_CHUNK, :],
                                   qcats[i][:, g * tq:(g + 1) * tq])
    s_next = [scores(*work[0], g) for g in heads]
    acc = den = None
    for w, (i, jc) in enumerate(work):
        s = s_next
        pf = [jnp.exp2(s[g]) for g in heads]
        ps = [jnp.sum(pf[g].reshape(KEY_CHUNK // 8, 8, tq), axis=0) for g in heads]
        s_next, pv = [], []
        for g in heads:
            if w + 1 < len(work):
                s_next.append(scores(*work[w + 1], g))
            pv.append(_dot(v_refs[jc[0]][0, jc[1]], pf[g].astype(BF16)))
        acc = pv if jc == chunks[0] else [acc[g] + pv[g] for g in heads]
        den = ps if jc == chunks[0] else [den[g] + ps[g] for g in heads]
        if jc == chunks[-1]:
            total = [jnp.broadcast_to(jnp.sum(den[g], axis=0, keepdims=True), den[g].shape) for g in heads]
            full = jnp.concatenate([jnp.concatenate(acc, axis=1), jnp.concatenate(total, axis=1)], axis=0)
            _attn_finish(full, o_ref, sub(i))


def _attn_kernel_online(q_ref, *refs, tq):
    k_refs, v_refs, o_ref = _attn_sources(refs)
    assert q_ref.shape[2] == tq
    qcat = _attn_queries(q_ref, slice(0, tq))
    ones = jnp.ones((16, KEY_CHUNK), BF16)
    carry = (jnp.full((1, qcat.shape[1]), -1e30, F32), jnp.zeros((HEAD_DIM + 16, qcat.shape[1]), F32))
    for k_ref, v_ref in zip(k_refs, v_refs):
        def step(c, carry, k_ref=k_ref, v_ref=v_ref):
            m_old, acc = carry
            start = pl.multiple_of(c * KEY_CHUNK, KEY_CHUNK)
            s = _dot(k_ref[0, 0, pl.ds(start, KEY_CHUNK), :], qcat)
            m_new = jnp.maximum(m_old, jnp.max(s, axis=0, keepdims=True))
            alpha = jnp.exp2(m_old - m_new)
            p = jnp.exp2((s - m_new).astype(BF16))
            va = jnp.concatenate([v_ref[0, c], ones], axis=0)
            return m_new, acc * alpha + _dot(va, p)

        carry = lax.fori_loop(0, k_ref.shape[2] // KEY_CHUNK, step, carry)
    _attn_finish(carry[1], o_ref, slice(0, tq))


def _attention(qt, ks, vs, logit_bound, *, tq):
    n_sub = max(m for m in (1, 2, 4) if qt.shape[2] % (m * tq) == 0)
    bounded = functools.partial(_attention_call, body=_attn_kernel_bounded, tq=tq, n_sub=n_sub)
    online = functools.partial(_attention_call, body=_attn_kernel_online, tq=tq, n_sub=1)
    return lax.cond(logit_bound <= MAX_UNSHIFTED_LOGIT, bounded, online, qt, *ks, *vs)


def _attention_call(qt, *kv, body, tq, n_sub):
    b, qw, s = qt.shape
    ks, vs = kv[:len(kv) // 2], kv[len(kv) // 2:]
    gq = GQA_GROUP * HEAD_DIM
    tb = n_sub * tq
    k_spec = lambda k: pl.BlockSpec((1, 1) + k.shape[2:], lambda bi, kv, i: (bi, kv, 0, 0))
    v_spec = lambda v: pl.BlockSpec((1, v.shape[1], HEAD_DIM, KEY_CHUNK), lambda bi, kv, i: (bi, 0, kv, 0))
    return pl.pallas_call(
        functools.partial(body, tq=tq),
        grid=(b, ks[0].shape[1], s // tb),
        in_specs=[pl.BlockSpec((1, gq, tb), lambda bi, kv, i: (bi, kv, i))]
                 + [k_spec(k) for k in ks] + [v_spec(v) for v in vs],
        out_specs=pl.BlockSpec((1, tb, gq), lambda bi, kv, i: (bi, i, kv)),
        out_shape=jax.ShapeDtypeStruct((b, s, qw), BF16),
        compiler_params=_params("arbitrary", "arbitrary", "arbitrary"),
        name="attention",
    )(qt, *ks, *vs)


def _fourier_kernel(chi_ref, shi_ref, clo_ref, slo_ref, mirror_ref, g_ref, lo_ref, hi_ref, wc_scr, ws_scr,
                    *, scale, hi_per_tile):
    n = clo_ref.shape[1]
    tm = hi_per_tile * DFT_RADIX
    i = pl.program_id(0)

    @pl.when(pl.program_id(1) == 0)
    def _():
        clo = clo_ref[...]
        slo = slo_ref[...]
        for j in range(hi_per_tile):
            chi = chi_ref[pl.ds(i * hi_per_tile + j, 1), :]
            shi = shi_ref[pl.ds(i * hi_per_tile + j, 1), :]
            rows = slice(j * DFT_RADIX, (j + 1) * DFT_RADIX)
            wc_scr[rows, :] = (chi * clo - shi * slo).astype(BF16)
            ws_scr[rows, :] = (shi * clo + chi * slo).astype(BF16)
        nxt = (i + 1) * hi_per_tile
        pad = wc_scr.shape[0] - tm
        wc_scr[tm:, :] = jnp.broadcast_to(chi_ref[pl.ds(nxt, 1), :], (pad, n)).astype(BF16)
        ws_scr[tm:, :] = jnp.broadcast_to(shi_ref[pl.ds(nxt, 1), :], (pad, n)).astype(BF16)

    prods = [(_dot(wc_scr[...], g_ref[j, 0:n, :]), _dot(ws_scr[...], g_ref[j, n:2 * n, :]))
             for j in range(g_ref.shape[0])]
    for j, (yc, ys) in enumerate(prods):
        lo_ref[j] = ((yc[:tm] - ys[:tm]) * scale).astype(BF16)
        hi_ref[j] = _dot(mirror_ref[...], ((yc + ys) * scale).astype(BF16)).astype(BF16)


def _dft_tables(n):
    hi = n // DFT_RADIX
    t = np.arange(n, dtype=np.int64)
    a_hi = (DFT_RADIX * np.arange(hi, dtype=np.int64)[:, None] * t[None, :]) % n
    a_lo = (np.arange(DFT_RADIX, dtype=np.int64)[:, None] * t[None, :]) % n
    f = lambda fn, a: jnp.asarray(fn(2.0 * np.pi * a / n), dtype=F32)
    return f(np.cos, a_hi), f(np.sin, a_hi), f(np.cos, a_lo), f(np.sin, a_lo)


def _fourier(g):
    b, n2, fw = g.shape
    n = n2 // 2
    chi, shi, clo, slo = _dft_tables(n)
    n_hi = n // DFT_RADIX
    hi_per_tile = min(8, n_hi // 2)
    tmf = hi_per_tile * DFT_RADIX
    n_tiles = n // (2 * tmf)
    rows_ext = tmf + 16
    mirror = np.zeros((tmf, rows_ext), np.float32)
    mirror[np.arange(tmf), tmf - np.arange(tmf)] = 1.0
    mirror = jnp.asarray(mirror).astype(BF16)
    const = lambda a: pl.BlockSpec(a.shape, lambda i, bi: (0, 0))
    kern = functools.partial(_fourier_kernel, scale=float(1.0 / np.sqrt(n * FOURIER_GROUP)),
                             hi_per_tile=hi_per_tile)
    half = jax.ShapeDtypeStruct((b, n // 2, fw), BF16)
    bb = 2 if b % 2 == 0 else 1
    lo, hi = pl.pallas_call(
        kern,
        grid=(n_tiles, b // bb),
        in_specs=[const(chi), const(shi), const(clo), const(slo), const(mirror),
                  pl.BlockSpec((bb, n2, fw), lambda i, bi: (bi, 0, 0))],
        out_specs=[pl.BlockSpec((bb, tmf, fw), lambda i, bi: (bi, i, 0)),
                   pl.BlockSpec((bb, tmf, fw), lambda i, bi: (bi, n_tiles - 1 - i, 0))],
        out_shape=[half, half],
        scratch_shapes=[pltpu.VMEM((rows_ext, n), BF16), pltpu.VMEM((rows_ext, n), BF16)],
        compiler_params=_params("arbitrary", "arbitrary"),
        name="fourier",
    )(chi, shi, clo, slo, mirror, g)
    return lo, hi


def _out_mlp_kernel(x_ref, a_ref, flo_ref, fhi_ref, m_ref, wo_ref, g2_ref, gt1_ref, sh_ref, sc_ref, gt2_ref,
                    w1_ref, w2_ref, o_ref, *, ff_chunk, lo_tiles):
    aw = a_ref.shape[2]
    fw = flo_ref.shape[2]
    if lo_tiles:
        four = jnp.where(pl.program_id(1) < lo_tiles, flo_ref[0], fhi_ref[0])
    else:
        four = jnp.concatenate([flo_ref[0], fhi_ref[0]], axis=0)
    gain2 = g2_ref[...] * (1.0 + sc_ref[0, 0])
    tm = x_ref.shape[1]
    sub = min(tm, KEY_CHUNK)

    def out_proj(rows):
        mix = (_dot(a_ref[0, rows, :], wo_ref[0:aw]) + _dot(four[rows], wo_ref[aw:aw + fw])
               + _dot(m_ref[0, rows, :], wo_ref[aw + fw:]))
        return x_ref[0, rows, :] + gt1_ref[0, 0] * mix

    def mlp(rows, x1):
        h = (_rms_rows(x1) * gain2 + sh_ref[0, 0]).astype(BF16)
        y = jnp.zeros_like(x1)
        for j in range(w1_ref.shape[1] // ff_chunk):
            cols = slice(j * ff_chunk, (j + 1) * ff_chunk)
            a = jnp.square(jnp.maximum(_dot(h, w1_ref[:, cols]), 0.0)).astype(BF16)
            y = y + _dot(a, w2_ref[cols, :])
        o_ref[0, rows, :] = x1 + gt2_ref[0, 0] * y

    tiles = [slice(j * sub, (j + 1) * sub) for j in range(tm // sub)]
    x1s = [out_proj(rows) for rows in tiles]
    for rows, x1 in zip(tiles, x1s):
        mlp(rows, x1)


def _out_mlp(x, attn, four, gm, wo, g2, mod, row_of, w1, w2, *, tm):
    b, n, d = x.shape
    f_lo, f_hi = four
    half = f_lo.shape[1]
    fw = f_lo.shape[2]
    tok = lambda w: pl.BlockSpec((1, tm, w), lambda bi, i: (bi, i, 0))
    if tm == n:
        lo_tiles = 0
        lo_spec = hi_spec = pl.BlockSpec((1, half, fw), lambda bi, i: (bi, 0, 0))
    else:
        lo_tiles = half // tm
        lo_spec = pl.BlockSpec((1, tm, fw), lambda bi, i: (bi, jnp.minimum(i, lo_tiles - 1), 0))
        hi_spec = pl.BlockSpec((1, tm, fw), lambda bi, i: (bi, jnp.maximum(i - lo_tiles, 0), 0))
    const = lambda a: pl.BlockSpec(a.shape, lambda bi, i: (0, 0), pipeline_mode=pl.Buffered(1))
    kern = functools.partial(_out_mlp_kernel, ff_chunk=1024, lo_tiles=lo_tiles)
    return pl.pallas_call(
        kern,
        grid=(b, n // tm),
        in_specs=[tok(d), tok(attn.shape[2]), lo_spec, hi_spec, tok(gm.shape[2]),
                  const(wo), pl.BlockSpec((1, d), lambda bi, i: (0, 0)),
                  *[_mod_spec(mod, row_of, term) for term in (2, 3, 4, 5)], const(w1), const(w2)],
        out_specs=tok(d),
        out_shape=jax.ShapeDtypeStruct((b, n, d), F32),
        compiler_params=_params("arbitrary", "arbitrary"),
        name="out_mlp",
    )(x, attn, f_lo, f_hi, gm, wo, g2, mod, mod, mod, mod, w1, w2)


def _rope_tables_t(n):
    rows = n // GRID_W
    row = jnp.repeat(jnp.arange(rows, dtype=F32), GRID_W)
    col = jnp.tile(jnp.arange(GRID_W, dtype=F32), rows)
    inv = ROPE_THETA ** (-jnp.arange(0, AXIS_DIM, 2, dtype=F32) / AXIS_DIM)
    ar = (row[:, None] * inv).T
    ac = (col[:, None] * inv).T
    ct = jnp.concatenate([jnp.cos(ar), jnp.cos(ar), jnp.cos(ac), jnp.cos(ac)], axis=0)
    st = jnp.concatenate([-jnp.sin(ar), jnp.sin(ar), -jnp.sin(ac), jnp.sin(ac)], axis=0)
    return ct, st


def _gained_rope(ct, st, gain, scale):
    h = ROT_HALF
    gain_sw = jnp.concatenate([gain[h:2 * h], gain[0:h], gain[3 * h:4 * h], gain[2 * h:3 * h]])
    return jnp.stack([ct * (gain * scale)[:, None], st * (gain_sw * scale)[:, None]])


def _channel_dft():
    c = np.arange(FOURIER_GROUP)
    ang = 2.0 * np.pi * ((c[:, None] * c[None, :]) % FOURIER_GROUP) / FOURIER_GROUP
    return np.cos(ang), np.sin(ang)


def kernel(x, c, ctx, c_ctx, w_ada, b_ada, norm1_g, norm2_g, w_in, q_norm_g, k_norm_g, gmlp_v_g,
           w_spatial, b_spatial, w_out, w_mlp1, w_mlp2):
    depth = w_ada.shape[0]
    b, s, d = x.shape
    n_ctx = ctx.shape[1]
    gw = gmlp_v_g.shape[1]
    n_gmlp_heads = gw // GMLP_HEAD
    fw = d // 4
    qkv_w = w_in.shape[2] - fw - 2 * gw

    cond = jnp.zeros((16, d), F32).at[:b].set(c).at[b].set(c_ctx)
    ct_x, st_x = _rope_tables_t(s)
    ct_c = jnp.ones((HEAD_DIM, n_ctx), F32)
    st_c = jnp.zeros((HEAD_DIM, n_ctx), F32)
    cc, sc = _channel_dft()
    eye = np.eye(fw // FOURIER_GROUP)
    cs = jnp.asarray(np.concatenate([np.kron(eye, cc), np.kron(eye, sc)], axis=1), dtype=F32).astype(BF16)

    tm_x = 512 if s % 512 == 0 else KEY_CHUNK
    tm_in = 1024 if s % 1024 == 0 else tm_x
    tm_c = KEY_CHUNK

    for l in range(depth):
        mod = _adaln(cond, w_ada, b_ada, l).reshape(cond.shape[0], 6, 1, d)
        x_row = lambda bi: bi
        c_row = lambda bi: b

        wt = w_in[l][:, :qkv_w].T.astype(BF16)
        wn = w_in[l][:, qkv_w:].astype(BF16)
        wts = dict(g1=norm1_g[l].reshape(1, d), wt=wt, wn=wn,
                   cs=cs, vg=gmlp_v_g[l].reshape(1, gw),
                   ws=w_spatial[l].transpose(1, 0, 2).reshape(CHUNK, n_gmlp_heads * CHUNK).astype(BF16),
                   bs=jnp.repeat(b_spatial[l].T, GMLP_HEAD, axis=1))
        rope_x = dict(rq=_gained_rope(ct_x, st_x, q_norm_g[l], Q_SCALE), rk=_gained_rope(ct_x, st_x, k_norm_g[l], 1.0))
        rope_c = dict(rq=_gained_rope(ct_c, st_c, q_norm_g[l], Q_SCALE), rk=_gained_rope(ct_c, st_c, k_norm_g[l], 1.0))
        q_x, k_x, v_x, g_x, gm_x = _in_proj(x, mod, x_row, tm=tm_in, **rope_x, **wts)
        q_c, k_c, v_c, g_c, gm_c = _in_proj(ctx, mod, c_row, tm=tm_c, **rope_c, **wts)

        logit_bound = (HEAD_DIM * Q_SCALE) * jnp.max(jnp.abs(q_norm_g[l])) * jnp.max(jnp.abs(k_norm_g[l]))
        a_x = _attention(q_x, (k_c, k_x), (v_c, v_x), logit_bound, tq=KEY_CHUNK)
        f_x = _fourier(g_x.reshape(b, 2 * s, fw))

        wo = w_out[l].astype(BF16)
        w1 = w_mlp1[l].astype(BF16)
        w2 = w_mlp2[l].astype(BF16)
        g2 = norm2_g[l].reshape(1, d)
        x = _out_mlp(x, a_x, f_x, gm_x, wo, g2, mod, x_row, w1, w2, tm=tm_in)

        if l < depth - 1:
            a_c = _attention(q_c, (k_c,), (v_c,), logit_bound, tq=KEY_CHUNK)
            f_c = _fourier(g_c.reshape(b, 2 * n_ctx, fw))
            ctx = _out_mlp(ctx, a_c, f_c, gm_c, wo, g2, mod, c_row, w1, w2, tm=tm_c)
    return x
```

```python
import functools

import numpy as np
import jax
import jax.numpy as jnp
from jax import lax
from jax.experimental import pallas as pl
from jax.experimental.pallas import tpu as pltpu

GRID_W = 64
HEAD_DIM = 64
AXIS_DIM = HEAD_DIM // 2
ROT_HALF = AXIS_DIM // 2
GQA_GROUP = 4
FOURIER_GROUP = 64
GMLP_HEAD = 64
CHUNK = 128
ROPE_THETA = 10000.0
EPS = 1e-6

VMEM_LIMIT_BYTES = 56 * 1024 * 1024
BF16_SUBLANES = 16
ADALN_COLS = 1536
FF_CHUNK = 1024
KEY_CHUNK = 256
DFT_RADIX = 64
LOG2_E = 1.4426950408889634
Q_SCALE = HEAD_DIM ** -0.5 * LOG2_E
MAX_UNSHIFTED_LOGIT = 60.0

BF16 = jnp.bfloat16
F32 = jnp.float32


def _params(*sem):
    return pltpu.CompilerParams(dimension_semantics=sem, vmem_limit_bytes=VMEM_LIMIT_BYTES)


def _dot(a, b):
    return jnp.dot(a, b, preferred_element_type=F32)


def _dot_nt(a, b):
    return lax.dot_general(a, b, (((1,), (1,)), ((), ())), preferred_element_type=F32)


def _rms_rows(x):
    return x * lax.rsqrt(jnp.mean(x * x, axis=-1, keepdims=True) + EPS)


def _adaln_kernel(c_ref, w_ref, b_ref, o_ref):
    h = jax.nn.silu(c_ref[...]).astype(BF16)
    o_ref[...] = _dot(h, w_ref[0].astype(BF16)) + b_ref[0]


def _adaln(cond, w, b, layer):
    rows, d = cond.shape
    depth, _, n = w.shape
    tn = ADALN_COLS
    return pl.pallas_call(
        _adaln_kernel,
        grid=(n // tn,),
        in_specs=[pl.BlockSpec((rows, d), lambda j: (0, 0)),
                  pl.BlockSpec((1, d, tn), lambda j: (layer, 0, j)),
                  pl.BlockSpec((1, 1, tn), lambda j: (layer, 0, j))],
        out_specs=pl.BlockSpec((rows, tn), lambda j: (0, j)),
        out_shape=jax.ShapeDtypeStruct((rows, n), F32),
        compiler_params=_params("arbitrary"),
        name="adaln",
    )(cond, w, b.reshape(depth, 1, n))


def _gelu_tanh(x):
    c = float(np.sqrt(2.0 / np.pi))
    hx = 0.5 * x
    return hx + hx * jnp.tanh(x * (c + (c * 0.044715) * (x * x)))


def _head_norm_rope_t(p, rope):
    r = lax.rsqrt(jnp.mean(p * p, axis=0, keepdims=True) + EPS)
    h = ROT_HALF
    psw = jnp.concatenate([p[h:2 * h], p[0:h], p[3 * h:4 * h], p[2 * h:3 * h]], axis=0)
    return (p * rope[0] + psw * rope[1]) * r


def _in_proj_kernel(x_ref, sh_ref, sc_ref, g1_ref, wt_ref, wn_ref, rq_ref, rk_ref,
                    cs_ref, vg_ref, ws_ref, bs_ref,
                    q_ref, k_ref, v_ref, g_ref, gm_ref, *, n_q_heads, n_kv_heads):
    tm = x_ref.shape[1]
    sub = KEY_CHUNK
    d = HEAD_DIM
    fw = cs_ref.shape[0]
    gw = vg_ref.shape[1]
    n_heads = gw // GMLP_HEAD
    gain = g1_ref[...] * (1.0 + sc_ref[0, 0])
    shift = sh_ref[0, 0]
    head_of_lane = lax.broadcasted_iota(jnp.int32, (1, gw), 1) // GMLP_HEAD

    def project(j):
        h = (_rms_rows(x_ref[0, j * sub:(j + 1) * sub, :]) * gain + shift).astype(BF16)
        return _dot_nt(wt_ref[...], h), _dot(h, wn_ref[...])

    def finish_heads(j, pt):
        tok = slice(j * sub, (j + 1) * sub)
        rq = rq_ref[:, :, tok]
        rk = rk_ref[:, :, tok]
        for i in range(n_q_heads):
            q_ref[0, i * d:(i + 1) * d, tok] = _head_norm_rope_t(pt[i * d:(i + 1) * d], rq).astype(BF16)
        ko = n_q_heads * d
        for i in range(n_kv_heads):
            kh = _head_norm_rope_t(pt[ko + i * d:ko + (i + 1) * d], rk)
            k_ref[0, i, tok, :] = jnp.concatenate([kh, jnp.zeros_like(kh)], axis=0).T.astype(BF16)
        vo = ko + n_kv_heads * d
        v_ref[0, j] = pt[vo:vo + n_kv_heads * d].astype(BF16)

    def finish_mixers(j, pn):
        tok = slice(j * sub, (j + 1) * sub)
        fcs = _dot(pn[:, :fw].astype(BF16), cs_ref[...])
        g_ref[0, 0, tok, :] = fcs[:, :fw].astype(BF16)
        g_ref[0, 1, tok, :] = fcs[:, fw:].astype(BF16)

        u = _gelu_tanh(pn[:, fw:fw + gw])
        gv = _gelu_tanh(pn[:, fw + gw:fw + 2 * gw])
        sq = gv * gv
        rinv = jnp.zeros_like(gv)
        for i in range(n_heads):
            m = head_of_lane == i
            ms = jnp.sum(jnp.where(m, sq, 0.0), axis=-1, keepdims=True) * (1.0 / GMLP_HEAD)
            rinv = jnp.where(m, lax.rsqrt(ms + EPS), rinv)
        vn = (gv * rinv * vg_ref[...]).astype(BF16)
        bs = bs_ref[...]
        for c in range(sub // CHUNK):
            vc = vn[c * CHUNK:(c + 1) * CHUNK]
            vblk = jnp.concatenate([jnp.where(head_of_lane == i, vc, jnp.zeros_like(vc))
                                    for i in range(n_heads)], axis=0)
            sp = bs + _dot(ws_ref[...], vblk)
            rows = slice(j * sub + c * CHUNK, j * sub + (c + 1) * CHUNK)
            gm_ref[0, rows, :] = (u[c * CHUNK:(c + 1) * CHUNK] * sp).astype(BF16)

    nxt = project(0)
    for j in range(tm // sub):
        pt, pn = nxt
        if j + 1 < tm // sub:
            nxt = project(j + 1)
        finish_heads(j, pt)
        finish_mixers(j, pn)


def _mod_spec(mod, row_of, term):
    return pl.BlockSpec((1, 1, 1, mod.shape[3]), lambda bi, i: (row_of(bi), term, 0, 0))


def _in_proj(x, mod, row_of, g1, wt, wn, rq, rk, cs, vg, ws, bs, *, tm):
    b, n, d = x.shape
    n_kv = wt.shape[0] // (HEAD_DIM * (GQA_GROUP + 2))
    n_q = n_kv * GQA_GROUP
    qw = n_q * HEAD_DIM
    kw = n_kv * HEAD_DIM
    fw = cs.shape[0]
    gw = vg.shape[1]
    cpt = tm // KEY_CHUNK
    const2 = lambda bi, i: (0, 0)
    kern = functools.partial(_in_proj_kernel, n_q_heads=n_q, n_kv_heads=n_kv)
    return pl.pallas_call(
        kern,
        grid=(b, n // tm),
        in_specs=[pl.BlockSpec((1, tm, d), lambda bi, i: (bi, i, 0)),
                  _mod_spec(mod, row_of, 0),
                  _mod_spec(mod, row_of, 1),
                  pl.BlockSpec((1, d), const2),
                  pl.BlockSpec(wt.shape, const2),
                  pl.BlockSpec(wn.shape, const2),
                  pl.BlockSpec((2, HEAD_DIM, tm), lambda bi, i: (0, 0, i)),
                  pl.BlockSpec((2, HEAD_DIM, tm), lambda bi, i: (0, 0, i)),
                  pl.BlockSpec(cs.shape, const2),
                  pl.BlockSpec((1, gw), const2),
                  pl.BlockSpec(ws.shape, const2),
                  pl.BlockSpec(bs.shape, const2)],
        out_specs=[pl.BlockSpec((1, qw, tm), lambda bi, i: (bi, 0, i)),
                   pl.BlockSpec((1, n_kv, tm, 2 * HEAD_DIM), lambda bi, i: (bi, 0, i, 0)),
                   pl.BlockSpec((1, cpt, kw, KEY_CHUNK), lambda bi, i: (bi, i, 0, 0)),
                   pl.BlockSpec((1, 2, tm, fw), lambda bi, i: (bi, 0, i, 0)),
                   pl.BlockSpec((1, tm, gw), lambda bi, i: (bi, i, 0))],
        out_shape=[jax.ShapeDtypeStruct((b, qw, n), BF16),
                   jax.ShapeDtypeStruct((b, n_kv, n, 2 * HEAD_DIM), BF16),
                   jax.ShapeDtypeStruct((b, n // KEY_CHUNK, kw, KEY_CHUNK), BF16),
                   jax.ShapeDtypeStruct((b, 2, n, fw), BF16),
                   jax.ShapeDtypeStruct((b, n, gw), BF16)],
        compiler_params=_params("arbitrary", "arbitrary"),
        name="in_proj",
    )(x, mod, mod, g1, wt, wn, rq, rk, cs, vg, ws, bs)


def _attn_queries(q_ref, cols):
    d = HEAD_DIM
    qcat = jnp.concatenate([q_ref[0, g * d:(g + 1) * d, cols] for g in range(GQA_GROUP)], axis=1)
    return jnp.concatenate([qcat, jnp.zeros_like(qcat)], axis=0)


def _attn_finish(acc, o_ref, rows):
    d = HEAD_DIM
    tq = acc.shape[1] // GQA_GROUP
    out = acc[:d] * (1.0 / acc[d:d + 1])
    for j in range(GQA_GROUP // 2):
        pair = jnp.concatenate([out[:, (2 * j) * tq:(2 * j + 1) * tq],
                                out[:, (2 * j + 1) * tq:(2 * j + 2) * tq]], axis=0)
        o_ref[0, rows, 2 * j * d:(2 * j + 2) * d] = pair.T.astype(BF16)


def _attn_sources(refs):
    n_src = (len(refs) - 1) // 2
    return refs[:n_src], refs[n_src:2 * n_src], refs[2 * n_src]


def _attn_kernel_bounded(q_ref, *refs, tq):
    k_refs, v_refs, o_ref = _attn_sources(refs)
    chunks = [(j, c) for j, kr in enumerate(k_refs) for c in range(kr.shape[2] // KEY_CHUNK)]
    n_sub = q_ref.shape[2] // tq
    sub = lambda i: slice(i * tq, (i + 1) * tq)
    heads = range(GQA_GROUP)
    qcats = [_attn_queries(q_ref, sub(i)) for i in range(n_sub)]
    work = [(i, jc) for i in range(n_sub) for jc in chunks]
    scores = lambda i, jc, g: _dot(k_refs[jc[0]][0, 0, jc[1] * KEY_CHUNK:(jc[1] + 1) * KEY_CHUNK, :],
                                   qcats[i][:, g * tq:(g + 1) * tq])
    s_next = [scores(*work[0], g) for g in heads]
    acc, den = {}, {}
    pending = None
    for w in range(len(work) + 1):
        if w < len(work):
            i, jc = work[w]
            s = s_next
            pf = [jnp.exp2(s[g]) for g in heads]
            ps = [jnp.sum(pf[g].reshape(KEY_CHUNK // 8, 8, tq), axis=0) for g in heads]
            den[i] = ps if jc == chunks[0] else [den[i][g] + ps[g] for g in heads]
        s_next, pv = [], []
        for g in heads:
            if w + 1 < len(work):
                s_next.append(scores(*work[w + 1], g))
            if pending is not None:
                (pi, pjc), p16 = pending
                pv.append(_dot(v_refs[pjc[0]][0, pjc[1]], p16[g]))
        if pending is not None:
            (pi, pjc), _ = pending
            acc[pi] = pv if pjc == chunks[0] else [acc[pi][g] + pv[g] for g in heads]
            if pjc == chunks[-1]:
                total = [jnp.broadcast_to(jnp.sum(den[pi][g], axis=0, keepdims=True), den[pi][g].shape)
                         for g in heads]
                full = jnp.concatenate([jnp.concatenate(acc[pi], axis=1), jnp.concatenate(total, axis=1)], axis=0)
                _attn_finish(full, o_ref, sub(pi))
        pending = (work[w], [pf[g].astype(BF16) for g in heads]) if w < len(work) else None


def _attn_kernel_online(q_ref, *refs, tq):
    k_refs, v_refs, o_ref = _attn_sources(refs)
    assert q_ref.shape[2] == tq
    qcat = _attn_queries(q_ref, slice(0, tq))
    ones = jnp.ones((BF16_SUBLANES, KEY_CHUNK), BF16)
    carry = (jnp.full((1, qcat.shape[1]), -1e30, F32),
             jnp.zeros((HEAD_DIM + BF16_SUBLANES, qcat.shape[1]), F32))
    for k_ref, v_ref in zip(k_refs, v_refs):
        def step(c, carry, k_ref=k_ref, v_ref=v_ref):
            m_old, acc = carry
            start = pl.multiple_of(c * KEY_CHUNK, KEY_CHUNK)
            s = _dot(k_ref[0, 0, pl.ds(start, KEY_CHUNK), :], qcat)
            m_new = jnp.maximum(m_old, jnp.max(s, axis=0, keepdims=True))
            alpha = jnp.exp2(m_old - m_new)
            p = jnp.exp2((s - m_new).astype(BF16))
            va = jnp.concatenate([v_ref[0, c], ones], axis=0)
            return m_new, acc * alpha + _dot(va, p)

        carry = lax.fori_loop(0, k_ref.shape[2] // KEY_CHUNK, step, carry)
    _attn_finish(carry[1], o_ref, slice(0, tq))


def _attention(qt, ks, vs, logit_bound, *, tq):
    n_sub = max(m for m in (1, 2, 4) if qt.shape[2] % (m * tq) == 0)
    bounded = functools.partial(_attention_call, body=_attn_kernel_bounded, tq=tq, n_sub=n_sub)
    online = functools.partial(_attention_call, body=_attn_kernel_online, tq=tq, n_sub=1)
    return lax.cond(logit_bound <= MAX_UNSHIFTED_LOGIT, bounded, online, qt, *ks, *vs)


def _attention_call(qt, *kv, body, tq, n_sub):
    b, qw, s = qt.shape
    ks, vs = kv[:len(kv) // 2], kv[len(kv) // 2:]
    gq = GQA_GROUP * HEAD_DIM
    tb = n_sub * tq
    k_spec = lambda k: pl.BlockSpec((1, 1) + k.shape[2:], lambda bi, kv, i: (bi, kv, 0, 0))
    v_spec = lambda v: pl.BlockSpec((1, v.shape[1], HEAD_DIM, KEY_CHUNK), lambda bi, kv, i: (bi, 0, kv, 0))
    return pl.pallas_call(
        functools.partial(body, tq=tq),
        grid=(b, ks[0].shape[1], s // tb),
        in_specs=[pl.BlockSpec((1, gq, tb), lambda bi, kv, i: (bi, kv, i))]
                 + [k_spec(k) for k in ks] + [v_spec(v) for v in vs],
        out_specs=pl.BlockSpec((1, tb, gq), lambda bi, kv, i: (bi, i, kv)),
        out_shape=jax.ShapeDtypeStruct((b, s, qw), BF16),
        compiler_params=_params("arbitrary", "arbitrary", "arbitrary"),
        name="attention",
    )(qt, *ks, *vs)


def _fourier_kernel(chi_ref, shi_ref, clo_ref, slo_ref, mirror_ref, g_ref, lo_ref, hi_ref, wc_scr, ws_scr,
                    *, scale, hi_per_tile):
    n = clo_ref.shape[1]
    tm = hi_per_tile * DFT_RADIX
    i = pl.program_id(0)

    @pl.when(pl.program_id(1) == 0)
    def _():
        clo = clo_ref[...]
        slo = slo_ref[...]
        for j in range(hi_per_tile):
            chi = chi_ref[pl.ds(i * hi_per_tile + j, 1), :]
            shi = shi_ref[pl.ds(i * hi_per_tile + j, 1), :]
            rows = slice(j * DFT_RADIX, (j + 1) * DFT_RADIX)
            wc_scr[rows, :] = (chi * clo - shi * slo).astype(BF16)
            ws_scr[rows, :] = (shi * clo + chi * slo).astype(BF16)
        nxt = (i + 1) * hi_per_tile
        pad = wc_scr.shape[0] - tm
        wc_scr[tm:, :] = jnp.broadcast_to(chi_ref[pl.ds(nxt, 1), :], (pad, n)).astype(BF16)
        ws_scr[tm:, :] = jnp.broadcast_to(shi_ref[pl.ds(nxt, 1), :], (pad, n)).astype(BF16)

    prods = [(_dot(wc_scr[...], g_ref[j, 0:n, :]), _dot(ws_scr[...], g_ref[j, n:2 * n, :]))
             for j in range(g_ref.shape[0])]
    for j, (yc, ys) in enumerate(prods):
        lo_ref[j] = ((yc[:tm] - ys[:tm]) * scale).astype(BF16)
        hi_ref[j] = _dot(mirror_ref[...], ((yc + ys) * scale).astype(BF16)).astype(BF16)


def _dft_tables(n):
    hi = n // DFT_RADIX
    t = np.arange(n, dtype=np.int64)
    a_hi = (DFT_RADIX * np.arange(hi, dtype=np.int64)[:, None] * t[None, :]) % n
    a_lo = (np.arange(DFT_RADIX, dtype=np.int64)[:, None] * t[None, :]) % n
    f = lambda fn, a: jnp.asarray(fn(2.0 * np.pi * a / n), dtype=F32)
    return f(np.cos, a_hi), f(np.sin, a_hi), f(np.cos, a_lo), f(np.sin, a_lo)


def _fourier(g):
    b, n2, fw = g.shape
    n = n2 // 2
    chi, shi, clo, slo = _dft_tables(n)
    n_hi = n // DFT_RADIX
    hi_per_tile = min(8, n_hi // 2)
    tmf = hi_per_tile * DFT_RADIX
    n_tiles = n // (2 * tmf)
    rows_ext = tmf + BF16_SUBLANES
    mirror = np.zeros((tmf, rows_ext), np.float32)
    mirror[np.arange(tmf), tmf - np.arange(tmf)] = 1.0
    mirror = jnp.asarray(mirror).astype(BF16)
    const = lambda a: pl.BlockSpec(a.shape, lambda i, bi: (0, 0))
    kern = functools.partial(_fourier_kernel, scale=float(1.0 / np.sqrt(n * FOURIER_GROUP)),
                             hi_per_tile=hi_per_tile)
    half = jax.ShapeDtypeStruct((b, n // 2, fw), BF16)
    bb = 2 if b % 2 == 0 else 1
    lo, hi = pl.pallas_call(
        kern,
        grid=(n_tiles, b // bb),
        in_specs=[const(chi), const(shi), const(clo), const(slo), const(mirror),
                  pl.BlockSpec((bb, n2, fw), lambda i, bi: (bi, 0, 0))],
        out_specs=[pl.BlockSpec((bb, tmf, fw), lambda i, bi: (bi, i, 0)),
                   pl.BlockSpec((bb, tmf, fw), lambda i, bi: (bi, n_tiles - 1 - i, 0))],
        out_shape=[half, half],
        scratch_shapes=[pltpu.VMEM((rows_ext, n), BF16), pltpu.VMEM((rows_ext, n), BF16)],
        compiler_params=_params("arbitrary", "arbitrary"),
        name="fourier",
    )(chi, shi, clo, slo, mirror, g)
    return lo, hi


def _out_mlp_kernel(x_ref, a_ref, flo_ref, fhi_ref, m_ref, wo_ref, g2_ref, gt1_ref, sh_ref, sc_ref, gt2_ref,
                    w1_ref, w2_ref, o_ref, *, ff_chunk, lo_tiles):
    aw = a_ref.shape[2]
    fw = flo_ref.shape[2]
    if lo_tiles:
        four = jnp.where(pl.program_id(1) < lo_tiles, flo_ref[0], fhi_ref[0])
    else:
        four = jnp.concatenate([flo_ref[0], fhi_ref[0]], axis=0)
    gain2 = g2_ref[...] * (1.0 + sc_ref[0, 0])
    tm = x_ref.shape[1]
    sub = min(tm, KEY_CHUNK)

    def out_proj(rows):
        mix = (_dot(a_ref[0, rows, :], wo_ref[0:aw]) + _dot(four[rows], wo_ref[aw:aw + fw])
               + _dot(m_ref[0, rows, :], wo_ref[aw + fw:]))
        return x_ref[0, rows, :] + gt1_ref[0, 0] * mix

    def mlp(rows, x1):
        h = (_rms_rows(x1) * gain2 + sh_ref[0, 0]).astype(BF16)
        y = jnp.zeros_like(x1)
        for j in range(w1_ref.shape[1] // ff_chunk):
            cols = slice(j * ff_chunk, (j + 1) * ff_chunk)
            a = jnp.square(jnp.maximum(_dot(h, w1_ref[:, cols]), 0.0)).astype(BF16)
            y = y + _dot(a, w2_ref[cols, :])
        o_ref[0, rows, :] = x1 + gt2_ref[0, 0] * y

    tiles = [slice(j * sub, (j + 1) * sub) for j in range(tm // sub)]
    x1s = [out_proj(rows) for rows in tiles]
    for rows, x1 in zip(tiles, x1s):
        mlp(rows, x1)


def _out_mlp(x, attn, four, gm, wo, g2, mod, row_of, w1, w2, *, tm):
    b, n, d = x.shape
    f_lo, f_hi = four
    half = f_lo.shape[1]
    fw = f_lo.shape[2]
    tok = lambda w: pl.BlockSpec((1, tm, w), lambda bi, i: (bi, i, 0))
    if tm == n:
        lo_tiles = 0
        lo_spec = hi_spec = pl.BlockSpec((1, half, fw), lambda bi, i: (bi, 0, 0))
    else:
        lo_tiles = half // tm
        lo_spec = pl.BlockSpec((1, tm, fw), lambda bi, i: (bi, jnp.minimum(i, lo_tiles - 1), 0))
        hi_spec = pl.BlockSpec((1, tm, fw), lambda bi, i: (bi, jnp.maximum(i - lo_tiles, 0), 0))
    const = lambda a: pl.BlockSpec(a.shape, lambda bi, i: (0, 0), pipeline_mode=pl.Buffered(1))
    kern = functools.partial(_out_mlp_kernel, ff_chunk=FF_CHUNK, lo_tiles=lo_tiles)
    return pl.pallas_call(
        kern,
        grid=(b, n // tm),
        in_specs=[tok(d), tok(attn.shape[2]), lo_spec, hi_spec, tok(gm.shape[2]),
                  const(wo), pl.BlockSpec((1, d), lambda bi, i: (0, 0)),
                  *[_mod_spec(mod, row_of, term) for term in (2, 3, 4, 5)], const(w1), const(w2)],
        out_specs=tok(d),
        out_shape=jax.ShapeDtypeStruct((b, n, d), F32),
        compiler_params=_params("arbitrary", "arbitrary"),
        name="out_mlp",
    )(x, attn, f_lo, f_hi, gm, wo, g2, mod, mod, mod, mod, w1, w2)


def _rope_tables_t(n):
    rows = n // GRID_W
    row = jnp.repeat(jnp.arange(rows, dtype=F32), GRID_W)
    col = jnp.tile(jnp.arange(GRID_W, dtype=F32), rows)
    inv = ROPE_THETA ** (-jnp.arange(0, AXIS_DIM, 2, dtype=F32) / AXIS_DIM)
    ar = (row[:, None] * inv).T
    ac = (col[:, None] * inv).T
    ct = jnp.concatenate([jnp.cos(ar), jnp.cos(ar), jnp.cos(ac), jnp.cos(ac)], axis=0)
    st = jnp.concatenate([-jnp.sin(ar), jnp.sin(ar), -jnp.sin(ac), jnp.sin(ac)], axis=0)
    return ct, st


def _gained_rope(ct, st, gain, scale):
    h = ROT_HALF
    gain_sw = jnp.concatenate([gain[h:2 * h], gain[0:h], gain[3 * h:4 * h], gain[2 * h:3 * h]])
    return jnp.stack([ct * (gain * scale)[:, None], st * (gain_sw * scale)[:, None]])


def _channel_dft():
    c = np.arange(FOURIER_GROUP)
    ang = 2.0 * np.pi * ((c[:, None] * c[None, :]) % FOURIER_GROUP) / FOURIER_GROUP
    return np.cos(ang), np.sin(ang)


def kernel(x, c, ctx, c_ctx, w_ada, b_ada, norm1_g, norm2_g, w_in, q_norm_g, k_norm_g, gmlp_v_g,
           w_spatial, b_spatial, w_out, w_mlp1, w_mlp2):
    depth = w_ada.shape[0]
    b, s, d = x.shape
    n_ctx = ctx.shape[1]
    gw = gmlp_v_g.shape[1]
    n_gmlp_heads = gw // GMLP_HEAD
    fw = d // 4
    qkv_w = w_in.shape[2] - fw - 2 * gw

    cond_rows = -(-(b + 1) // BF16_SUBLANES) * BF16_SUBLANES
    cond = jnp.zeros((cond_rows, d), F32).at[:b].set(c).at[b].set(c_ctx)
    ct_x, st_x = _rope_tables_t(s)
    ct_c = jnp.ones((HEAD_DIM, n_ctx), F32)
    st_c = jnp.zeros((HEAD_DIM, n_ctx), F32)
    cc, sc = _channel_dft()
    eye = np.eye(fw // FOURIER_GROUP)
    cs = jnp.asarray(np.concatenate([np.kron(eye, cc), np.kron(eye, sc)], axis=1), dtype=F32).astype(BF16)

    tm_x = 512 if s % 512 == 0 else KEY_CHUNK
    tm_in = 1024 if s % 1024 == 0 else tm_x
    tm_c = KEY_CHUNK

    for l in range(depth):
        mod = _adaln(cond, w_ada, b_ada, l).reshape(cond.shape[0], 6, 1, d)
        x_row = lambda bi: bi
        c_row = lambda bi: b

        wt = w_in[l][:, :qkv_w].T.astype(BF16)
        wn = w_in[l][:, qkv_w:].astype(BF16)
        wts = dict(g1=norm1_g[l].reshape(1, d), wt=wt, wn=wn,
                   cs=cs, vg=gmlp_v_g[l].reshape(1, gw),
                   ws=w_spatial[l].transpose(1, 0, 2).reshape(CHUNK, n_gmlp_heads * CHUNK).astype(BF16),
                   bs=jnp.repeat(b_spatial[l].T, GMLP_HEAD, axis=1))
        rope_x = dict(rq=_gained_rope(ct_x, st_x, q_norm_g[l], Q_SCALE), rk=_gained_rope(ct_x, st_x, k_norm_g[l], 1.0))
        rope_c = dict(rq=_gained_rope(ct_c, st_c, q_norm_g[l], Q_SCALE), rk=_gained_rope(ct_c, st_c, k_norm_g[l], 1.0))
        q_x, k_x, v_x, g_x, gm_x = _in_proj(x, mod, x_row, tm=tm_in, **rope_x, **wts)
        q_c, k_c, v_c, g_c, gm_c = _in_proj(ctx, mod, c_row, tm=tm_c, **rope_c, **wts)

        logit_bound = (HEAD_DIM * Q_SCALE) * jnp.max(jnp.abs(q_norm_g[l])) * jnp.max(jnp.abs(k_norm_g[l]))
        a_x = _attention(q_x, (k_c, k_x), (v_c, v_x), logit_bound, tq=KEY_CHUNK)
        f_x = _fourier(g_x.reshape(b, 2 * s, fw))

        wo = w_out[l].astype(BF16)
        w1 = w_mlp1[l].astype(BF16)
        w2 = w_mlp2[l].astype(BF16)
        g2 = norm2_g[l].reshape(1, d)
        x = _out_mlp(x, a_x, f_x, gm_x, wo, g2, mod, x_row, w1, w2, tm=tm_in)

        if l < depth - 1:
            a_c = _attention(q_c, (k_c,), (v_c,), logit_bound, tq=KEY_CHUNK)
            f_c = _fourier(g_c.reshape(b, 2 * n_ctx, fw))
            ctx = _out_mlp(ctx, a_c, f_c, gm_c, wo, g2, mod, c_row, w1, w2, tm=tm_c)
    return x
```

```python
import functools

import numpy as np
import jax
import jax.numpy as jnp
from jax import lax
from jax.experimental import pallas as pl
from jax.experimental.pallas import tpu as pltpu

GRID_W = 64
HEAD_DIM = 64
AXIS_DIM = HEAD_DIM // 2
ROT_HALF = AXIS_DIM // 2
GQA_GROUP = 4
FOURIER_GROUP = 64
GMLP_HEAD = 64
CHUNK = 128
ROPE_THETA = 10000.0
EPS = 1e-6

VMEM_LIMIT_BYTES = 56 * 1024 * 1024
BF16_SUBLANES = 16
ADALN_COLS = 1536
FF_CHUNK = 1024
PV_LAG = 1
KEY_CHUNK = 256
DFT_RADIX = 64
LOG2_E = 1.4426950408889634
Q_SCALE = HEAD_DIM ** -0.5 * LOG2_E
MAX_UNSHIFTED_LOGIT = 60.0

BF16 = jnp.bfloat16
F32 = jnp.float32


def _params(*sem):
    return pltpu.CompilerParams(dimension_semantics=sem, vmem_limit_bytes=VMEM_LIMIT_BYTES)


def _dot(a, b):
    return jnp.dot(a, b, preferred_element_type=F32)


def _dot_nt(a, b):
    return lax.dot_general(a, b, (((1,), (1,)), ((), ())), preferred_element_type=F32)


def _rms_rows(x):
    return x * lax.rsqrt(jnp.mean(x * x, axis=-1, keepdims=True) + EPS)


def _adaln_kernel(c_ref, w_ref, b_ref, o_ref):
    h = jax.nn.silu(c_ref[...]).astype(BF16)
    o_ref[...] = _dot(h, w_ref[0].astype(BF16)) + b_ref[0]


def _adaln(cond, w, b, layer):
    rows, d = cond.shape
    depth, _, n = w.shape
    tn = ADALN_COLS
    return pl.pallas_call(
        _adaln_kernel,
        grid=(n // tn,),
        in_specs=[pl.BlockSpec((rows, d), lambda j: (0, 0)),
                  pl.BlockSpec((1, d, tn), lambda j: (layer, 0, j)),
                  pl.BlockSpec((1, 1, tn), lambda j: (layer, 0, j))],
        out_specs=pl.BlockSpec((rows, tn), lambda j: (0, j)),
        out_shape=jax.ShapeDtypeStruct((rows, n), F32),
        compiler_params=_params("arbitrary"),
        name="adaln",
    )(cond, w, b.reshape(depth, 1, n))


def _gelu_tanh(x):
    c = float(np.sqrt(2.0 / np.pi))
    hx = 0.5 * x
    return hx + hx * jnp.tanh(x * (c + (c * 0.044715) * (x * x)))


def _head_norm_rope_t(p, rope):
    r = lax.rsqrt(jnp.mean(p * p, axis=0, keepdims=True) + EPS)
    h = ROT_HALF
    psw = jnp.concatenate([p[h:2 * h], p[0:h], p[3 * h:4 * h], p[2 * h:3 * h]], axis=0)
    return (p * rope[0] + psw * rope[1]) * r


def _in_proj_kernel(x_ref, sh_ref, sc_ref, g1_ref, wt_ref, wn_ref, rq_ref, rk_ref,
                    cs_ref, vg_ref, ws_ref, bs_ref,
                    q_ref, k_ref, v_ref, g_ref, gm_ref, *, n_q_heads, n_kv_heads):
    tm = x_ref.shape[1]
    sub = KEY_CHUNK
    d = HEAD_DIM
    fw = cs_ref.shape[0]
    gw = vg_ref.shape[1]
    n_heads = gw // GMLP_HEAD
    gain = g1_ref[...] * (1.0 + sc_ref[0, 0])
    shift = sh_ref[0, 0]
    head_of_lane = lax.broadcasted_iota(jnp.int32, (1, gw), 1) // GMLP_HEAD

    def project(j):
        h = (_rms_rows(x_ref[0, j * sub:(j + 1) * sub, :]) * gain + shift).astype(BF16)
        return _dot_nt(wt_ref[...], h), _dot(h, wn_ref[...])

    def finish_heads(j, pt):
        tok = slice(j * sub, (j + 1) * sub)
        rq = rq_ref[:, :, tok]
        rk = rk_ref[:, :, tok]
        for i in range(n_q_heads):
            q_ref[0, i * d:(i + 1) * d, tok] = _head_norm_rope_t(pt[i * d:(i + 1) * d], rq).astype(BF16)
        ko = n_q_heads * d
        for i in range(n_kv_heads):
            kh = _head_norm_rope_t(pt[ko + i * d:ko + (i + 1) * d], rk)
            k_ref[0, i, tok, :] = jnp.concatenate([kh, jnp.zeros_like(kh)], axis=0).T.astype(BF16)
        vo = ko + n_kv_heads * d
        v_ref[0, j] = pt[vo:vo + n_kv_heads * d].astype(BF16)

    def finish_mixers(j, pn):
        u = _gelu_tanh(pn[:, fw:fw + gw])
        gv = _gelu_tanh(pn[:, fw + gw:fw + 2 * gw])
        sq = gv * gv
        rinv = jnp.zeros_like(gv)
        for i in range(n_heads):
            m = head_of_lane == i
            ms = jnp.sum(jnp.where(m, sq, 0.0), axis=-1, keepdims=True) * (1.0 / GMLP_HEAD)
            rinv = jnp.where(m, lax.rsqrt(ms + EPS), rinv)
        vn = (gv * rinv * vg_ref[...]).astype(BF16)
        vblks = []
        for c in range(sub // CHUNK):
            vc = vn[c * CHUNK:(c + 1) * CHUNK]
            vblks.append(jnp.concatenate([jnp.where(head_of_lane == i, vc, jnp.zeros_like(vc))
                                          for i in range(n_heads)], axis=0))
        return j, pn[:, :fw].astype(BF16), u, vblks

    def finish_mixer_matmuls(j, f16, u, vblks):
        tok = slice(j * sub, (j + 1) * sub)
        fcs = _dot(f16, cs_ref[...])
        g_ref[0, 0, tok, :] = fcs[:, :fw].astype(BF16)
        g_ref[0, 1, tok, :] = fcs[:, fw:].astype(BF16)
        for c, vblk in enumerate(vblks):
            sp = bs_ref[...] + _dot(ws_ref[...], vblk)
            rows = slice(j * sub + c * CHUNK, j * sub + (c + 1) * CHUNK)
            gm_ref[0, rows, :] = (u[c * CHUNK:(c + 1) * CHUNK] * sp).astype(BF16)

    nxt = project(0)
    lagging = None
    for j in range(tm // sub):
        pt, pn = nxt
        if j + 1 < tm // sub:
            nxt = project(j + 1)
        finish_heads(j, pt)
        ready = finish_mixers(j, pn)
        if lagging is not None:
            finish_mixer_matmuls(*lagging)
        lagging = ready
    finish_mixer_matmuls(*lagging)


def _mod_spec(mod, row_of, term):
    return pl.BlockSpec((1, 1, 1, mod.shape[3]), lambda bi, i: (row_of(bi), term, 0, 0))


def _in_proj(x, mod, row_of, g1, wt, wn, rq, rk, cs, vg, ws, bs, *, tm):
    b, n, d = x.shape
    n_kv = wt.shape[0] // (HEAD_DIM * (GQA_GROUP + 2))
    n_q = n_kv * GQA_GROUP
    qw = n_q * HEAD_DIM
    kw = n_kv * HEAD_DIM
    fw = cs.shape[0]
    gw = vg.shape[1]
    cpt = tm // KEY_CHUNK
    const2 = lambda bi, i: (0, 0)
    kern = functools.partial(_in_proj_kernel, n_q_heads=n_q, n_kv_heads=n_kv)
    return pl.pallas_call(
        kern,
        grid=(b, n // tm),
        in_specs=[pl.BlockSpec((1, tm, d), lambda bi, i: (bi, i, 0)),
                  _mod_spec(mod, row_of, 0),
                  _mod_spec(mod, row_of, 1),
                  pl.BlockSpec((1, d), const2),
                  pl.BlockSpec(wt.shape, const2),
                  pl.BlockSpec(wn.shape, const2),
                  pl.BlockSpec((2, HEAD_DIM, tm), lambda bi, i: (0, 0, i)),
                  pl.BlockSpec((2, HEAD_DIM, tm), lambda bi, i: (0, 0, i)),
                  pl.BlockSpec(cs.shape, const2),
                  pl.BlockSpec((1, gw), const2),
                  pl.BlockSpec(ws.shape, const2),
                  pl.BlockSpec(bs.shape, const2)],
        out_specs=[pl.BlockSpec((1, qw, tm), lambda bi, i: (bi, 0, i)),
                   pl.BlockSpec((1, n_kv, tm, 2 * HEAD_DIM), lambda bi, i: (bi, 0, i, 0)),
                   pl.BlockSpec((1, cpt, kw, KEY_CHUNK), lambda bi, i: (bi, i, 0, 0)),
                   pl.BlockSpec((1, 2, tm, fw), lambda bi, i: (bi, 0, i, 0)),
                   pl.BlockSpec((1, tm, gw), lambda bi, i: (bi, i, 0))],
        out_shape=[jax.ShapeDtypeStruct((b, qw, n), BF16),
                   jax.ShapeDtypeStruct((b, n_kv, n, 2 * HEAD_DIM), BF16),
                   jax.ShapeDtypeStruct((b, n // KEY_CHUNK, kw, KEY_CHUNK), BF16),
                   jax.ShapeDtypeStruct((b, 2, n, fw), BF16),
                   jax.ShapeDtypeStruct((b, n, gw), BF16)],
        compiler_params=_params("arbitrary", "arbitrary"),
        name="in_proj",
    )(x, mod, mod, g1, wt, wn, rq, rk, cs, vg, ws, bs)


def _attn_queries(q_ref, cols):
    d = HEAD_DIM
    qcat = jnp.concatenate([q_ref[0, g * d:(g + 1) * d, cols] for g in range(GQA_GROUP)], axis=1)
    return jnp.concatenate([qcat, jnp.zeros_like(qcat)], axis=0)


def _attn_finish(acc, o_ref, rows):
    d = HEAD_DIM
    tq = acc.shape[1] // GQA_GROUP
    out = acc[:d] * (1.0 / acc[d:d + 1])
    for j in range(GQA_GROUP // 2):
        pair = jnp.concatenate([out[:, (2 * j) * tq:(2 * j + 1) * tq],
                                out[:, (2 * j + 1) * tq:(2 * j + 2) * tq]], axis=0)
        o_ref[0, rows, 2 * j * d:(2 * j + 2) * d] = pair.T.astype(BF16)


def _attn_sources(refs):
    n_src = (len(refs) - 1) // 2
    return refs[:n_src], refs[n_src:2 * n_src], refs[2 * n_src]


def _attn_kernel_bounded(q_ref, *refs, tq):
    k_refs, v_refs, o_ref = _attn_sources(refs)
    chunks = [(j, c) for j, kr in enumerate(k_refs) for c in range(kr.shape[2] // KEY_CHUNK)]
    n_sub = q_ref.shape[2] // tq
    sub = lambda i: slice(i * tq, (i + 1) * tq)
    heads = range(GQA_GROUP)
    qcats = [_attn_queries(q_ref, sub(i)) for i in range(n_sub)]
    work = [(i, jc) for i in range(n_sub) for jc in chunks]
    scores = lambda i, jc, g: _dot(k_refs[jc[0]][0, 0, jc[1] * KEY_CHUNK:(jc[1] + 1) * KEY_CHUNK, :],
                                   qcats[i][:, g * tq:(g + 1) * tq])
    s_next = [scores(*work[0], g) for g in heads]
    acc, den = {}, {}
    queue = []
    for w in range(len(work) + PV_LAG):
        pending = queue.pop(0) if len(queue) == PV_LAG or (w >= len(work) and queue) else None
        if w < len(work):
            i, jc = work[w]
            s = s_next
            pf = [jnp.exp2(s[g]) for g in heads]
            ps = [jnp.sum(pf[g].reshape(KEY_CHUNK // 8, 8, tq), axis=0) for g in heads]
            den[i] = ps if jc == chunks[0] else [den[i][g] + ps[g] for g in heads]
        s_next, pv = [], []
        for g in heads:
            if w + 1 < len(work):
                s_next.append(scores(*work[w + 1], g))
            if pending is not None:
                (pi, pjc), p16 = pending
                pv.append(_dot(v_refs[pjc[0]][0, pjc[1]], p16[g]))
        if pending is not None:
            (pi, pjc), _ = pending
            acc[pi] = pv if pjc == chunks[0] else [acc[pi][g] + pv[g] for g in heads]
            if pjc == chunks[-1]:
                total = [jnp.broadcast_to(jnp.sum(den[pi][g], axis=0, keepdims=True), den[pi][g].shape)
                         for g in heads]
                full = jnp.concatenate([jnp.concatenate(acc[pi], axis=1), jnp.concatenate(total, axis=1)], axis=0)
                _attn_finish(full, o_ref, sub(pi))
        if w < len(work):
            queue.append((work[w], [pf[g].astype(BF16) for g in heads]))


def _attn_kernel_online(q_ref, *refs, tq):
    k_refs, v_refs, o_ref = _attn_sources(refs)
    assert q_ref.shape[2] == tq
    qcat = _attn_queries(q_ref, slice(0, tq))
    ones = jnp.ones((BF16_SUBLANES, KEY_CHUNK), BF16)
    carry = (jnp.full((1, qcat.shape[1]), -1e30, F32),
             jnp.zeros((HEAD_DIM + BF16_SUBLANES, qcat.shape[1]), F32))
    for k_ref, v_ref in zip(k_refs, v_refs):
        def step(c, carry, k_ref=k_ref, v_ref=v_ref):
            m_old, acc = carry
            start = pl.multiple_of(c * KEY_CHUNK, KEY_CHUNK)
            s = _dot(k_ref[0, 0, pl.ds(start, KEY_CHUNK), :], qcat)
            m_new = jnp.maximum(m_old, jnp.max(s, axis=0, keepdims=True))
            alpha = jnp.exp2(m_old - m_new)
            p = jnp.exp2((s - m_new).astype(BF16))
            va = jnp.concatenate([v_ref[0, c], ones], axis=0)
            return m_new, acc * alpha + _dot(va, p)

        carry = lax.fori_loop(0, k_ref.shape[2] // KEY_CHUNK, step, carry)
    _attn_finish(carry[1], o_ref, slice(0, tq))


def _attention(qt, ks, vs, logit_bound, *, tq):
    n_sub = max(m for m in (1, 2, 4) if qt.shape[2] % (m * tq) == 0)
    bounded = functools.partial(_attention_call, body=_attn_kernel_bounded, tq=tq, n_sub=n_sub)
    online = functools.partial(_attention_call, body=_attn_kernel_online, tq=tq, n_sub=1)
    return lax.cond(logit_bound <= MAX_UNSHIFTED_LOGIT, bounded, online, qt, *ks, *vs)


def _attention_call(qt, *kv, body, tq, n_sub):
    b, qw, s = qt.shape
    ks, vs = kv[:len(kv) // 2], kv[len(kv) // 2:]
    gq = GQA_GROUP * HEAD_DIM
    tb = n_sub * tq
    k_spec = lambda k: pl.BlockSpec((1, 1) + k.shape[2:], lambda bi, kv, i: (bi, kv, 0, 0))
    v_spec = lambda v: pl.BlockSpec((1, v.shape[1], HEAD_DIM, KEY_CHUNK), lambda bi, kv, i: (bi, 0, kv, 0))
    return pl.pallas_call(
        functools.partial(body, tq=tq),
        grid=(b, ks[0].shape[1], s // tb),
        in_specs=[pl.BlockSpec((1, gq, tb), lambda bi, kv, i: (bi, kv, i))]
                 + [k_spec(k) for k in ks] + [v_spec(v) for v in vs],
        out_specs=pl.BlockSpec((1, tb, gq), lambda bi, kv, i: (bi, i, kv)),
        out_shape=jax.ShapeDtypeStruct((b, s, qw), BF16),
        compiler_params=_params("arbitrary", "arbitrary", "arbitrary"),
        name="attention",
    )(qt, *ks, *vs)


def _fourier_kernel(chi_ref, shi_ref, clo_ref, slo_ref, mirror_ref, g_ref, lo_ref, hi_ref, wc_scr, ws_scr,
                    *, scale, hi_per_tile):
    n = clo_ref.shape[1]
    tm = hi_per_tile * DFT_RADIX
    i = pl.program_id(0)

    @pl.when(pl.program_id(1) == 0)
    def _():
        clo = clo_ref[...]
        slo = slo_ref[...]
        for j in range(hi_per_tile):
            chi = chi_ref[pl.ds(i * hi_per_tile + j, 1), :]
            shi = shi_ref[pl.ds(i * hi_per_tile + j, 1), :]
            rows = slice(j * DFT_RADIX, (j + 1) * DFT_RADIX)
            wc_scr[rows, :] = (chi * clo - shi * slo).astype(BF16)
            ws_scr[rows, :] = (shi * clo + chi * slo).astype(BF16)
        nxt = (i + 1) * hi_per_tile
        pad = wc_scr.shape[0] - tm
        wc_scr[tm:, :] = jnp.broadcast_to(chi_ref[pl.ds(nxt, 1), :], (pad, n)).astype(BF16)
        ws_scr[tm:, :] = jnp.broadcast_to(shi_ref[pl.ds(nxt, 1), :], (pad, n)).astype(BF16)

    prods = [(_dot(wc_scr[...], g_ref[j, 0:n, :]), _dot(ws_scr[...], g_ref[j, n:2 * n, :]))
             for j in range(g_ref.shape[0])]
    for j, (yc, ys) in enumerate(prods):
        lo_ref[j] = ((yc[:tm] - ys[:tm]) * scale).astype(BF16)
        hi_ref[j] = _dot(mirror_ref[...], ((yc + ys) * scale).astype(BF16)).astype(BF16)


def _dft_tables(n):
    hi = n // DFT_RADIX
    t = np.arange(n, dtype=np.int64)
    a_hi = (DFT_RADIX * np.arange(hi, dtype=np.int64)[:, None] * t[None, :]) % n
    a_lo = (np.arange(DFT_RADIX, dtype=np.int64)[:, None] * t[None, :]) % n
    f = lambda fn, a: jnp.asarray(fn(2.0 * np.pi * a / n), dtype=F32)
    return f(np.cos, a_hi), f(np.sin, a_hi), f(np.cos, a_lo), f(np.sin, a_lo)


def _fourier(g):
    b, n2, fw = g.shape
    n = n2 // 2
    chi, shi, clo, slo = _dft_tables(n)
    n_hi = n // DFT_RADIX
    hi_per_tile = min(8, n_hi // 2)
    tmf = hi_per_tile * DFT_RADIX
    n_tiles = n // (2 * tmf)
    rows_ext = tmf + BF16_SUBLANES
    mirror = np.zeros((tmf, rows_ext), np.float32)
    mirror[np.arange(tmf), tmf - np.arange(tmf)] = 1.0
    mirror = jnp.asarray(mirror).astype(BF16)
    const = lambda a: pl.BlockSpec(a.shape, lambda i, bi: (0, 0))
    kern = functools.partial(_fourier_kernel, scale=float(1.0 / np.sqrt(n * FOURIER_GROUP)),
                             hi_per_tile=hi_per_tile)
    half = jax.ShapeDtypeStruct((b, n // 2, fw), BF16)
    bb = 2 if b % 2 == 0 else 1
    lo, hi = pl.pallas_call(
        kern,
        grid=(n_tiles, b // bb),
        in_specs=[const(chi), const(shi), const(clo), const(slo), const(mirror),
                  pl.BlockSpec((bb, n2, fw), lambda i, bi: (bi, 0, 0))],
        out_specs=[pl.BlockSpec((bb, tmf, fw), lambda i, bi: (bi, i, 0)),
                   pl.BlockSpec((bb, tmf, fw), lambda i, bi: (bi, n_tiles - 1 - i, 0))],
        out_shape=[half, half],
        scratch_shapes=[pltpu.VMEM((rows_ext, n), BF16), pltpu.VMEM((rows_ext, n), BF16)],
        compiler_params=_params("arbitrary", "arbitrary"),
        name="fourier",
    )(chi, shi, clo, slo, mirror, g)
    return lo, hi


def _out_mlp_kernel(x_ref, a_ref, flo_ref, fhi_ref, m_ref, wo_ref, g2_ref, gt1_ref, sh_ref, sc_ref, gt2_ref,
                    w1_ref, w2_ref, o_ref, *, ff_chunk, lo_tiles):
    aw = a_ref.shape[2]
    fw = flo_ref.shape[2]
    if lo_tiles:
        four = jnp.where(pl.program_id(1) < lo_tiles, flo_ref[0], fhi_ref[0])
    else:
        four = jnp.concatenate([flo_ref[0], fhi_ref[0]], axis=0)
    gain2 = g2_ref[...] * (1.0 + sc_ref[0, 0])
    tm = x_ref.shape[1]
    sub = min(tm, KEY_CHUNK)

    def out_proj(rows):
        mix = (_dot(a_ref[0, rows, :], wo_ref[0:aw]) + _dot(four[rows], wo_ref[aw:aw + fw])
               + _dot(m_ref[0, rows, :], wo_ref[aw + fw:]))
        return x_ref[0, rows, :] + gt1_ref[0, 0] * mix

    def mlp(rows, x1):
        h = (_rms_rows(x1) * gain2 + sh_ref[0, 0]).astype(BF16)
        y = jnp.zeros_like(x1)
        for j in range(w1_ref.shape[1] // ff_chunk):
            cols = slice(j * ff_chunk, (j + 1) * ff_chunk)
            a = jnp.square(jnp.maximum(_dot(h, w1_ref[:, cols]), 0.0)).astype(BF16)
            y = y + _dot(a, w2_ref[cols, :])
        o_ref[0, rows, :] = x1 + gt2_ref[0, 0] * y

    tiles = [slice(j * sub, (j + 1) * sub) for j in range(tm // sub)]
    x1s = [out_proj(rows) for rows in tiles]
    for rows, x1 in zip(tiles, x1s):
        mlp(rows, x1)


def _out_mlp(x, attn, four, gm, wo, g2, mod, row_of, w1, w2, *, tm):
    b, n, d = x.shape
    f_lo, f_hi = four
    half = f_lo.shape[1]
    fw = f_lo.shape[2]
    tok = lambda w: pl.BlockSpec((1, tm, w), lambda bi, i: (bi, i, 0))
    if tm == n:
        lo_tiles = 0
        lo_spec = hi_spec = pl.BlockSpec((1, half, fw), lambda bi, i: (bi, 0, 0))
    else:
        lo_tiles = half // tm
        lo_spec = pl.BlockSpec((1, tm, fw), lambda bi, i: (bi, jnp.minimum(i, lo_tiles - 1), 0))
        hi_spec = pl.BlockSpec((1, tm, fw), lambda bi, i: (bi, jnp.maximum(i - lo_tiles, 0), 0))
    const = lambda a: pl.BlockSpec(a.shape, lambda bi, i: (0, 0), pipeline_mode=pl.Buffered(1))
    kern = functools.partial(_out_mlp_kernel, ff_chunk=FF_CHUNK, lo_tiles=lo_tiles)
    return pl.pallas_call(
        kern,
        grid=(b, n // tm),
        in_specs=[tok(d), tok(attn.shape[2]), lo_spec, hi_spec, tok(gm.shape[2]),
                  const(wo), pl.BlockSpec((1, d), lambda bi, i: (0, 0)),
                  *[_mod_spec(mod, row_of, term) for term in (2, 3, 4, 5)], const(w1), const(w2)],
        out_specs=tok(d),
        out_shape=jax.ShapeDtypeStruct((b, n, d), F32),
        compiler_params=_params("arbitrary", "arbitrary"),
        name="out_mlp",
    )(x, attn, f_lo, f_hi, gm, wo, g2, mod, mod, mod, mod, w1, w2)


def _rope_tables_t(n):
    rows = n // GRID_W
    row = jnp.repeat(jnp.arange(rows, dtype=F32), GRID_W)
    col = jnp.tile(jnp.arange(GRID_W, dtype=F32), rows)
    inv = ROPE_THETA ** (-jnp.arange(0, AXIS_DIM, 2, dtype=F32) / AXIS_DIM)
    ar = (row[:, None] * inv).T
    ac = (col[:, None] * inv).T
    ct = jnp.concatenate([jnp.cos(ar), jnp.cos(ar), jnp.cos(ac), jnp.cos(ac)], axis=0)
    st = jnp.concatenate([-jnp.sin(ar), jnp.sin(ar), -jnp.sin(ac), jnp.sin(ac)], axis=0)
    return ct, st


def _gained_rope(ct, st, gain, scale):
    h = ROT_HALF
    gain_sw = jnp.concatenate([gain[h:2 * h], gain[0:h], gain[3 * h:4 * h], gain[2 * h:3 * h]])
    return jnp.stack([ct * (gain * scale)[:, None], st * (gain_sw * scale)[:, None]])


def _channel_dft():
    c = np.arange(FOURIER_GROUP)
    ang = 2.0 * np.pi * ((c[:, None] * c[None, :]) % FOURIER_GROUP) / FOURIER_GROUP
    return np.cos(ang), np.sin(ang)


def kernel(x, c, ctx, c_ctx, w_ada, b_ada, norm1_g, norm2_g, w_in, q_norm_g, k_norm_g, gmlp_v_g,
           w_spatial, b_spatial, w_out, w_mlp1, w_mlp2):
    depth = w_ada.shape[0]
    b, s, d = x.shape
    n_ctx = ctx.shape[1]
    gw = gmlp_v_g.shape[1]
    n_gmlp_heads = gw // GMLP_HEAD
    fw = d // 4
    qkv_w = w_in.shape[2] - fw - 2 * gw

    cond_rows = -(-(b + 1) // BF16_SUBLANES) * BF16_SUBLANES
    cond = jnp.zeros((cond_rows, d), F32).at[:b].set(c).at[b].set(c_ctx)
    ct_x, st_x = _rope_tables_t(s)
    ct_c = jnp.ones((HEAD_DIM, n_ctx), F32)
    st_c = jnp.zeros((HEAD_DIM, n_ctx), F32)
    cc, sc = _channel_dft()
    eye = np.eye(fw // FOURIER_GROUP)
    cs = jnp.asarray(np.concatenate([np.kron(eye, cc), np.kron(eye, sc)], axis=1), dtype=F32).astype(BF16)

    tm_x = 512 if s % 512 == 0 else KEY_CHUNK
    tm_in = 1024 if s % 1024 == 0 else tm_x
    tm_c = KEY_CHUNK

    for l in range(depth):
        mod = _adaln(cond, w_ada, b_ada, l).reshape(cond.shape[0], 6, 1, d)
        x_row = lambda bi: bi
        c_row = lambda bi: b

        wt = w_in[l][:, :qkv_w].T.astype(BF16)
        wn = w_in[l][:, qkv_w:].astype(BF16)
        wts = dict(g1=norm1_g[l].reshape(1, d), wt=wt, wn=wn,
                   cs=cs, vg=gmlp_v_g[l].reshape(1, gw),
                   ws=w_spatial[l].transpose(1, 0, 2).reshape(CHUNK, n_gmlp_heads * CHUNK).astype(BF16),
                   bs=jnp.repeat(b_spatial[l].T, GMLP_HEAD, axis=1))
        rope_x = dict(rq=_gained_rope(ct_x, st_x, q_norm_g[l], Q_SCALE), rk=_gained_rope(ct_x, st_x, k_norm_g[l], 1.0))
        rope_c = dict(rq=_gained_rope(ct_c, st_c, q_norm_g[l], Q_SCALE), rk=_gained_rope(ct_c, st_c, k_norm_g[l], 1.0))
        q_x, k_x, v_x, g_x, gm_x = _in_proj(x, mod, x_row, tm=tm_in, **rope_x, **wts)
        q_c, k_c, v_c, g_c, gm_c = _in_proj(ctx, mod, c_row, tm=tm_c, **rope_c, **wts)

        logit_bound = (HEAD_DIM * Q_SCALE) * jnp.max(jnp.abs(q_norm_g[l])) * jnp.max(jnp.abs(k_norm_g[l]))
        a_x = _attention(q_x, (k_c, k_x), (v_c, v_x), logit_bound, tq=KEY_CHUNK)
        f_x = _fourier(g_x.reshape(b, 2 * s, fw))

        wo = w_out[l].astype(BF16)
        w1 = w_mlp1[l].astype(BF16)
        w2 = w_mlp2[l].astype(BF16)
        g2 = norm2_g[l].reshape(1, d)
        x = _out_mlp(x, a_x, f_x, gm_x, wo, g2, mod, x_row, w1, w2, tm=tm_in)

        if l < depth - 1:
            a_c = _attention(q_c, (k_c,), (v_c,), logit_bound, tq=KEY_CHUNK)
            f_c = _fourier(g_c.reshape(b, 2 * n_ctx, fw))
            ctx = _out_mlp(ctx, a_c, f_c, gm_c, wo, g2, mod, c_row, w1, w2, tm=tm_c)
    return x
```

```python
import functools

import numpy as np
import jax
import jax.numpy as jnp
from jax import lax
from jax.experimental import pallas as pl
from jax.experimental.pallas import tpu as pltpu

GRID_W = 64
HEAD_DIM = 64
AXIS_DIM = HEAD_DIM // 2
ROT_HALF = AXIS_DIM // 2
GQA_GROUP = 4
FOURIER_GROUP = 64
GMLP_HEAD = 64
CHUNK = 128
ROPE_THETA = 10000.0
EPS = 1e-6

VMEM_LIMIT_BYTES = 56 * 1024 * 1024
BF16_SUBLANES = 16
ADALN_COLS = 1536
FF_CHUNK = 1024
PV_LAG = 1
KEY_CHUNK = 256
DFT_RADIX = 64
LOG2_E = 1.4426950408889634
Q_SCALE = HEAD_DIM ** -0.5 * LOG2_E
MAX_UNSHIFTED_LOGIT = 60.0

BF16 = jnp.bfloat16
F32 = jnp.float32


def _params(*sem):
    return pltpu.CompilerParams(dimension_semantics=sem, vmem_limit_bytes=VMEM_LIMIT_BYTES)


def _dot(a, b):
    return jnp.dot(a, b, preferred_element_type=F32)


def _dot_nt(a, b):
    return lax.dot_general(a, b, (((1,), (1,)), ((), ())), preferred_element_type=F32)


def _rms_rows(x):
    return x * lax.rsqrt(jnp.mean(x * x, axis=-1, keepdims=True) + EPS)


def _adaln_kernel(c_ref, w_ref, b_ref, o_ref):
    h = jax.nn.silu(c_ref[...]).astype(BF16)
    o_ref[...] = _dot(h, w_ref[0].astype(BF16)) + b_ref[0]


def _adaln(cond, w, b, layer):
    rows, d = cond.shape
    depth, _, n = w.shape
    tn = ADALN_COLS
    return pl.pallas_call(
        _adaln_kernel,
        grid=(n // tn,),
        in_specs=[pl.BlockSpec((rows, d), lambda j: (0, 0)),
                  pl.BlockSpec((1, d, tn), lambda j: (layer, 0, j)),
                  pl.BlockSpec((1, 1, tn), lambda j: (layer, 0, j))],
        out_specs=pl.BlockSpec((rows, tn), lambda j: (0, j)),
        out_shape=jax.ShapeDtypeStruct((rows, n), F32),
        compiler_params=_params("arbitrary"),
        name="adaln",
    )(cond, w, b.reshape(depth, 1, n))


def _gelu_tanh(x):
    c = float(np.sqrt(2.0 / np.pi))
    hx = 0.5 * x
    return hx + hx * jnp.tanh(x * (c + (c * 0.044715) * (x * x)))


def _head_norm_rope_t(p, rope):
    r = lax.rsqrt(jnp.mean(p * p, axis=0, keepdims=True) + EPS)
    h = ROT_HALF
    psw = jnp.concatenate([p[h:2 * h], p[0:h], p[3 * h:4 * h], p[2 * h:3 * h]], axis=0)
    return (p * rope[0] + psw * rope[1]) * r


def _in_proj_kernel(x_ref, sh_ref, sc_ref, g1_ref, w_ref, rq_ref, rk_ref,
                    cs_ref, vg_ref, ws_ref, bs_ref,
                    q_ref, k_ref, v_ref, g_ref, gm_ref, wt_ref, wn_ref, *, n_q_heads, n_kv_heads):
    tm = x_ref.shape[1]
    sub = KEY_CHUNK
    d = HEAD_DIM
    fw = cs_ref.shape[0]
    gw = vg_ref.shape[1]

    @pl.when((pl.program_id(0) == 0) & (pl.program_id(1) == 0))
    def _():
        qkv_w = wt_ref.shape[0]
        for j in range(qkv_w // KEY_CHUNK):
            cols = slice(j * KEY_CHUNK, (j + 1) * KEY_CHUNK)
            wt_ref[cols, :] = w_ref[0, :, cols].T.astype(BF16)
        wn_ref[...] = w_ref[0, :, qkv_w:].astype(BF16)

    n_heads = gw // GMLP_HEAD
    gain = g1_ref[...] * (1.0 + sc_ref[0, 0])
    shift = sh_ref[0, 0]
    head_of_lane = lax.broadcasted_iota(jnp.int32, (1, gw), 1) // GMLP_HEAD

    def project(j):
        h = (_rms_rows(x_ref[0, j * sub:(j + 1) * sub, :]) * gain + shift).astype(BF16)
        return _dot_nt(wt_ref[...], h), _dot(h, wn_ref[...])

    def finish_heads(j, pt):
        tok = slice(j * sub, (j + 1) * sub)
        rq = rq_ref[:, :, tok]
        rk = rk_ref[:, :, tok]
        for i in range(n_q_heads):
            q_ref[0, i * d:(i + 1) * d, tok] = _head_norm_rope_t(pt[i * d:(i + 1) * d], rq).astype(BF16)
        ko = n_q_heads * d
        for i in range(n_kv_heads):
            kh = _head_norm_rope_t(pt[ko + i * d:ko + (i + 1) * d], rk)
            k_ref[0, i, tok, :] = jnp.concatenate([kh, jnp.zeros_like(kh)], axis=0).T.astype(BF16)
        vo = ko + n_kv_heads * d
        v_ref[0, j] = pt[vo:vo + n_kv_heads * d].astype(BF16)

    def finish_mixers(j, pn):
        u = _gelu_tanh(pn[:, fw:fw + gw])
        gv = _gelu_tanh(pn[:, fw + gw:fw + 2 * gw])
        sq = gv * gv
        rinv = jnp.zeros_like(gv)
        for i in range(n_heads):
            m = head_of_lane == i
            ms = jnp.sum(jnp.where(m, sq, 0.0), axis=-1, keepdims=True) * (1.0 / GMLP_HEAD)
            rinv = jnp.where(m, lax.rsqrt(ms + EPS), rinv)
        vn = (gv * rinv * vg_ref[...]).astype(BF16)
        vblks = []
        for c in range(sub // CHUNK):
            vc = vn[c * CHUNK:(c + 1) * CHUNK]
            vblks.append(jnp.concatenate([jnp.where(head_of_lane == i, vc, jnp.zeros_like(vc))
                                          for i in range(n_heads)], axis=0))
        return j, pn[:, :fw].astype(BF16), u, vblks

    def finish_mixer_matmuls(j, f16, u, vblks):
        tok = slice(j * sub, (j + 1) * sub)
        fcs = _dot(f16, cs_ref[...])
        g_ref[0, 0, tok, :] = fcs[:, :fw].astype(BF16)
        g_ref[0, 1, tok, :] = fcs[:, fw:].astype(BF16)
        for c, vblk in enumerate(vblks):
            sp = bs_ref[...] + _dot(ws_ref[...], vblk)
            rows = slice(j * sub + c * CHUNK, j * sub + (c + 1) * CHUNK)
            gm_ref[0, rows, :] = (u[c * CHUNK:(c + 1) * CHUNK] * sp).astype(BF16)

    nxt = project(0)
    lagging = None
    for j in range(tm // sub):
        pt, pn = nxt
        if j + 1 < tm // sub:
            nxt = project(j + 1)
        finish_heads(j, pt)
        ready = finish_mixers(j, pn)
        if lagging is not None:
            finish_mixer_matmuls(*lagging)
        lagging = ready
    finish_mixer_matmuls(*lagging)


def _mod_spec(mod, row_of, term):
    return pl.BlockSpec((1, 1, 1, mod.shape[3]), lambda bi, i: (row_of(bi), term, 0, 0))


def _in_proj(x, mod, row_of, g1, w_in, layer, qkv_w, rq, rk, cs, vg, ws, bs, *, tm):
    b, n, d = x.shape
    in_width = w_in.shape[2]
    n_kv = qkv_w // (HEAD_DIM * (GQA_GROUP + 2))
    n_q = n_kv * GQA_GROUP
    qw = n_q * HEAD_DIM
    kw = n_kv * HEAD_DIM
    fw = cs.shape[0]
    gw = vg.shape[1]
    cpt = tm // KEY_CHUNK
    const2 = lambda bi, i: (0, 0)
    kern = functools.partial(_in_proj_kernel, n_q_heads=n_q, n_kv_heads=n_kv)
    return pl.pallas_call(
        kern,
        grid=(b, n // tm),
        in_specs=[pl.BlockSpec((1, tm, d), lambda bi, i: (bi, i, 0)),
                  _mod_spec(mod, row_of, 0),
                  _mod_spec(mod, row_of, 1),
                  pl.BlockSpec((1, d), const2),
                  pl.BlockSpec((1, d, in_width), lambda bi, i: (layer, 0, 0)),
                  pl.BlockSpec((2, HEAD_DIM, tm), lambda bi, i: (0, 0, i)),
                  pl.BlockSpec((2, HEAD_DIM, tm), lambda bi, i: (0, 0, i)),
                  pl.BlockSpec(cs.shape, const2),
                  pl.BlockSpec((1, gw), const2),
                  pl.BlockSpec(ws.shape, const2),
                  pl.BlockSpec(bs.shape, const2)],
        out_specs=[pl.BlockSpec((1, qw, tm), lambda bi, i: (bi, 0, i)),
                   pl.BlockSpec((1, n_kv, tm, 2 * HEAD_DIM), lambda bi, i: (bi, 0, i, 0)),
                   pl.BlockSpec((1, cpt, kw, KEY_CHUNK), lambda bi, i: (bi, i, 0, 0)),
                   pl.BlockSpec((1, 2, tm, fw), lambda bi, i: (bi, 0, i, 0)),
                   pl.BlockSpec((1, tm, gw), lambda bi, i: (bi, i, 0))],
        out_shape=[jax.ShapeDtypeStruct((b, qw, n), BF16),
                   jax.ShapeDtypeStruct((b, n_kv, n, 2 * HEAD_DIM), BF16),
                   jax.ShapeDtypeStruct((b, n // KEY_CHUNK, kw, KEY_CHUNK), BF16),
                   jax.ShapeDtypeStruct((b, 2, n, fw), BF16),
                   jax.ShapeDtypeStruct((b, n, gw), BF16)],
        scratch_shapes=[pltpu.VMEM((qkv_w, d), BF16), pltpu.VMEM((d, in_width - qkv_w), BF16)],
        compiler_params=_params("arbitrary", "arbitrary"),
        name="in_proj",
    )(x, mod, mod, g1, w_in, rq, rk, cs, vg, ws, bs)


def _attn_queries(q_ref, cols):
    d = HEAD_DIM
    qcat = jnp.concatenate([q_ref[0, g * d:(g + 1) * d, cols] for g in range(GQA_GROUP)], axis=1)
    return jnp.concatenate([qcat, jnp.zeros_like(qcat)], axis=0)


def _attn_finish(acc, o_ref, rows):
    d = HEAD_DIM
    tq = acc.shape[1] // GQA_GROUP
    out = acc[:d] * (1.0 / acc[d:d + 1])
    for j in range(GQA_GROUP // 2):
        pair = jnp.concatenate([out[:, (2 * j) * tq:(2 * j + 1) * tq],
                                out[:, (2 * j + 1) * tq:(2 * j + 2) * tq]], axis=0)
        o_ref[0, rows, 2 * j * d:(2 * j + 2) * d] = pair.T.astype(BF16)


def _attn_sources(refs):
    n_src = (len(refs) - 1) // 2
    return refs[:n_src], refs[n_src:2 * n_src], refs[2 * n_src]


def _attn_kernel_bounded(q_ref, *refs, tq):
    k_refs, v_refs, o_ref = _attn_sources(refs)
    chunks = [(j, c) for j, kr in enumerate(k_refs) for c in range(kr.shape[2] // KEY_CHUNK)]
    n_sub = q_ref.shape[2] // tq
    sub = lambda i: slice(i * tq, (i + 1) * tq)
    heads = range(GQA_GROUP)
    qcats = [_attn_queries(q_ref, sub(i)) for i in range(n_sub)]
    work = [(i, jc) for i in range(n_sub) for jc in chunks]
    scores = lambda i, jc, g: _dot(k_refs[jc[0]][0, 0, jc[1] * KEY_CHUNK:(jc[1] + 1) * KEY_CHUNK, :],
                                   qcats[i][:, g * tq:(g + 1) * tq])
    s_next = [scores(*work[0], g) for g in heads]
    acc, den = {}, {}
    queue = []
    for w in range(len(work) + PV_LAG):
        pending = queue.pop(0) if len(queue) == PV_LAG or (w >= len(work) and queue) else None
        if w < len(work):
            i, jc = work[w]
            s = s_next
            pf = [jnp.exp2(s[g]) for g in heads]
            ps = [jnp.sum(pf[g].reshape(KEY_CHUNK // 8, 8, tq), axis=0) for g in heads]
            den[i] = ps if jc == chunks[0] else [den[i][g] + ps[g] for g in heads]
        s_next, pv = [], []
        for g in heads:
            if w + 1 < len(work):
                s_next.append(scores(*work[w + 1], g))
            if pending is not None:
                (pi, pjc), p16 = pending
                pv.append(_dot(v_refs[pjc[0]][0, pjc[1]], p16[g]))
        if pending is not None:
            (pi, pjc), _ = pending
            acc[pi] = pv if pjc == chunks[0] else [acc[pi][g] + pv[g] for g in heads]
            if pjc == chunks[-1]:
                total = [jnp.broadcast_to(jnp.sum(den[pi][g], axis=0, keepdims=True), den[pi][g].shape)
                         for g in heads]
                full = jnp.concatenate([jnp.concatenate(acc[pi], axis=1), jnp.concatenate(total, axis=1)], axis=0)
                _attn_finish(full, o_ref, sub(pi))
        if w < len(work):
            queue.append((work[w], [pf[g].astype(BF16) for g in heads]))


def _attn_kernel_online(q_ref, *refs, tq):
    k_refs, v_refs, o_ref = _attn_sources(refs)
    assert q_ref.shape[2] == tq
    qcat = _attn_queries(q_ref, slice(0, tq))
    ones = jnp.ones((BF16_SUBLANES, KEY_CHUNK), BF16)
    carry = (jnp.full((1, qcat.shape[1]), -1e30, F32),
             jnp.zeros((HEAD_DIM + BF16_SUBLANES, qcat.shape[1]), F32))
    for k_ref, v_ref in zip(k_refs, v_refs):
        def step(c, carry, k_ref=k_ref, v_ref=v_ref):
            m_old, acc = carry
            start = pl.multiple_of(c * KEY_CHUNK, KEY_CHUNK)
            s = _dot(k_ref[0, 0, pl.ds(start, KEY_CHUNK), :], qcat)
            m_new = jnp.maximum(m_old, jnp.max(s, axis=0, keepdims=True))
            alpha = jnp.exp2(m_old - m_new)
            p = jnp.exp2((s - m_new).astype(BF16))
            va = jnp.concatenate([v_ref[0, c], ones], axis=0)
            return m_new, acc * alpha + _dot(va, p)

        carry = lax.fori_loop(0, k_ref.shape[2] // KEY_CHUNK, step, carry)
    _attn_finish(carry[1], o_ref, slice(0, tq))


def _attention(qt, ks, vs, logit_bound, *, tq):
    n_sub = max(m for m in (1, 2, 4) if qt.shape[2] % (m * tq) == 0)
    bounded = functools.partial(_attention_call, body=_attn_kernel_bounded, tq=tq, n_sub=n_sub)
    online = functools.partial(_attention_call, body=_attn_kernel_online, tq=tq, n_sub=1)
    return lax.cond(logit_bound <= MAX_UNSHIFTED_LOGIT, bounded, online, qt, *ks, *vs)


def _attention_call(qt, *kv, body, tq, n_sub):
    b, qw, s = qt.shape
    ks, vs = kv[:len(kv) // 2], kv[len(kv) // 2:]
    gq = GQA_GROUP * HEAD_DIM
    tb = n_sub * tq
    k_spec = lambda k: pl.BlockSpec((1, 1) + k.shape[2:], lambda bi, kv, i: (bi, kv, 0, 0))
    v_spec = lambda v: pl.BlockSpec((1, v.shape[1], HEAD_DIM, KEY_CHUNK), lambda bi, kv, i: (bi, 0, kv, 0))
    return pl.pallas_call(
        functools.partial(body, tq=tq),
        grid=(b, ks[0].shape[1], s // tb),
        in_specs=[pl.BlockSpec((1, gq, tb), lambda bi, kv, i: (bi, kv, i))]
                 + [k_spec(k) for k in ks] + [v_spec(v) for v in vs],
        out_specs=pl.BlockSpec((1, tb, gq), lambda bi, kv, i: (bi, i, kv)),
        out_shape=jax.ShapeDtypeStruct((b, s, qw), BF16),
        compiler_params=_params("arbitrary", "arbitrary", "arbitrary"),
        name="attention",
    )(qt, *ks, *vs)


def _fourier_kernel(chi_ref, shi_ref, clo_ref, slo_ref, mirror_ref, g_ref, lo_ref, hi_ref, wc_scr, ws_scr,
                    *, scale, hi_per_tile):
    n = clo_ref.shape[1]
    tm = hi_per_tile * DFT_RADIX
    i = pl.program_id(0)

    @pl.when(pl.program_id(1) == 0)
    def _():
        clo = clo_ref[...]
        slo = slo_ref[...]
        for j in range(hi_per_tile):
            chi = chi_ref[pl.ds(i * hi_per_tile + j, 1), :]
            shi = shi_ref[pl.ds(i * hi_per_tile + j, 1), :]
            rows = slice(j * DFT_RADIX, (j + 1) * DFT_RADIX)
            wc_scr[rows, :] = (chi * clo - shi * slo).astype(BF16)
            ws_scr[rows, :] = (shi * clo + chi * slo).astype(BF16)
        nxt = (i + 1) * hi_per_tile
        pad = wc_scr.shape[0] - tm
        wc_scr[tm:, :] = jnp.broadcast_to(chi_ref[pl.ds(nxt, 1), :], (pad, n)).astype(BF16)
        ws_scr[tm:, :] = jnp.broadcast_to(shi_ref[pl.ds(nxt, 1), :], (pad, n)).astype(BF16)

    prods = [(_dot(wc_scr[...], g_ref[j, 0:n, :]), _dot(ws_scr[...], g_ref[j, n:2 * n, :]))
             for j in range(g_ref.shape[0])]
    for j, (yc, ys) in enumerate(prods):
        lo_ref[j] = ((yc[:tm] - ys[:tm]) * scale).astype(BF16)
        hi_ref[j] = _dot(mirror_ref[...], ((yc + ys) * scale).astype(BF16)).astype(BF16)


def _dft_tables(n):
    hi = n // DFT_RADIX
    t = np.arange(n, dtype=np.int64)
    a_hi = (DFT_RADIX * np.arange(hi, dtype=np.int64)[:, None] * t[None, :]) % n
    a_lo = (np.arange(DFT_RADIX, dtype=np.int64)[:, None] * t[None, :]) % n
    f = lambda fn, a: jnp.asarray(fn(2.0 * np.pi * a / n), dtype=F32)
    return f(np.cos, a_hi), f(np.sin, a_hi), f(np.cos, a_lo), f(np.sin, a_lo)


def _fourier(g):
    b, n2, fw = g.shape
    n = n2 // 2
    chi, shi, clo, slo = _dft_tables(n)
    n_hi = n // DFT_RADIX
    hi_per_tile = min(8, n_hi // 2)
    tmf = hi_per_tile * DFT_RADIX
    n_tiles = n // (2 * tmf)
    rows_ext = tmf + BF16_SUBLANES
    mirror = np.zeros((tmf, rows_ext), np.float32)
    mirror[np.arange(tmf), tmf - np.arange(tmf)] = 1.0
    mirror = jnp.asarray(mirror).astype(BF16)
    const = lambda a: pl.BlockSpec(a.shape, lambda i, bi: (0, 0))
    kern = functools.partial(_fourier_kernel, scale=float(1.0 / np.sqrt(n * FOURIER_GROUP)),
                             hi_per_tile=hi_per_tile)
    half = jax.ShapeDtypeStruct((b, n // 2, fw), BF16)
    bb = 2 if b % 2 == 0 else 1
    lo, hi = pl.pallas_call(
        kern,
        grid=(n_tiles, b // bb),
        in_specs=[const(chi), const(shi), const(clo), const(slo), const(mirror),
                  pl.BlockSpec((bb, n2, fw), lambda i, bi: (bi, 0, 0))],
        out_specs=[pl.BlockSpec((bb, tmf, fw), lambda i, bi: (bi, i, 0)),
                   pl.BlockSpec((bb, tmf, fw), lambda i, bi: (bi, n_tiles - 1 - i, 0))],
        out_shape=[half, half],
        scratch_shapes=[pltpu.VMEM((rows_ext, n), BF16), pltpu.VMEM((rows_ext, n), BF16)],
        compiler_params=_params("arbitrary", "arbitrary"),
        name="fourier",
    )(chi, shi, clo, slo, mirror, g)
    return lo, hi


def _out_mlp_kernel(x_ref, a_ref, flo_ref, fhi_ref, m_ref, wo_ref, g2_ref, gt1_ref, sh_ref, sc_ref, gt2_ref,
                    w1_ref, w2_ref, o_ref, *, ff_chunk, lo_tiles):
    aw = a_ref.shape[2]
    fw = flo_ref.shape[2]
    if lo_tiles:
        four = jnp.where(pl.program_id(1) < lo_tiles, flo_ref[0], fhi_ref[0])
    else:
        four = jnp.concatenate([flo_ref[0], fhi_ref[0]], axis=0)
    gain2 = g2_ref[...] * (1.0 + sc_ref[0, 0])
    tm = x_ref.shape[1]
    sub = min(tm, KEY_CHUNK)

    def out_proj(rows):
        mix = (_dot(a_ref[0, rows, :], wo_ref[0:aw]) + _dot(four[rows], wo_ref[aw:aw + fw])
               + _dot(m_ref[0, rows, :], wo_ref[aw + fw:]))
        return x_ref[0, rows, :] + gt1_ref[0, 0] * mix

    def mlp(rows, x1):
        h = (_rms_rows(x1) * gain2 + sh_ref[0, 0]).astype(BF16)
        y = jnp.zeros_like(x1)
        for j in range(w1_ref.shape[1] // ff_chunk):
            cols = slice(j * ff_chunk, (j + 1) * ff_chunk)
            a = jnp.square(jnp.maximum(_dot(h, w1_ref[:, cols]), 0.0)).astype(BF16)
            y = y + _dot(a, w2_ref[cols, :])
        o_ref[0, rows, :] = x1 + gt2_ref[0, 0] * y

    tiles = [slice(j * sub, (j + 1) * sub) for j in range(tm // sub)]
    x1s = [out_proj(rows) for rows in tiles]
    for rows, x1 in zip(tiles, x1s):
        mlp(rows, x1)


def _out_mlp(x, attn, four, gm, wo, g2, mod, row_of, w1, w2, *, tm):
    b, n, d = x.shape
    f_lo, f_hi = four
    half = f_lo.shape[1]
    fw = f_lo.shape[2]
    tok = lambda w: pl.BlockSpec((1, tm, w), lambda bi, i: (bi, i, 0))
    if tm == n:
        lo_tiles = 0
        lo_spec = hi_spec = pl.BlockSpec((1, half, fw), lambda bi, i: (bi, 0, 0))
    else:
        lo_tiles = half // tm
        lo_spec = pl.BlockSpec((1, tm, fw), lambda bi, i: (bi, jnp.minimum(i, lo_tiles - 1), 0))
        hi_spec = pl.BlockSpec((1, tm, fw), lambda bi, i: (bi, jnp.maximum(i - lo_tiles, 0), 0))
    const = lambda a: pl.BlockSpec(a.shape, lambda bi, i: (0, 0), pipeline_mode=pl.Buffered(1))
    kern = functools.partial(_out_mlp_kernel, ff_chunk=FF_CHUNK, lo_tiles=lo_tiles)
    return pl.pallas_call(
        kern,
        grid=(b, n // tm),
        in_specs=[tok(d), tok(attn.shape[2]), lo_spec, hi_spec, tok(gm.shape[2]),
                  const(wo), pl.BlockSpec((1, d), lambda bi, i: (0, 0)),
                  *[_mod_spec(mod, row_of, term) for term in (2, 3, 4, 5)], const(w1), const(w2)],
        out_specs=tok(d),
        out_shape=jax.ShapeDtypeStruct((b, n, d), F32),
        compiler_params=_params("arbitrary", "arbitrary"),
        name="out_mlp",
    )(x, attn, f_lo, f_hi, gm, wo, g2, mod, mod, mod, mod, w1, w2)


def _rope_tables_t(n):
    rows = n // GRID_W
    row = jnp.repeat(jnp.arange(rows, dtype=F32), GRID_W)
    col = jnp.tile(jnp.arange(GRID_W, dtype=F32), rows)
    inv = ROPE_THETA ** (-jnp.arange(0, AXIS_DIM, 2, dtype=F32) / AXIS_DIM)
    ar = (row[:, None] * inv).T
    ac = (col[:, None] * inv).T
    ct = jnp.concatenate([jnp.cos(ar), jnp.cos(ar), jnp.cos(ac), jnp.cos(ac)], axis=0)
    st = jnp.concatenate([-jnp.sin(ar), jnp.sin(ar), -jnp.sin(ac), jnp.sin(ac)], axis=0)
    return ct, st


def _gained_rope(ct, st, gain, scale):
    h = ROT_HALF
    gain_sw = jnp.concatenate([gain[h:2 * h], gain[0:h], gain[3 * h:4 * h], gain[2 * h:3 * h]])
    return jnp.stack([ct * (gain * scale)[:, None], st * (gain_sw * scale)[:, None]])


def _channel_dft():
    c = np.arange(FOURIER_GROUP)
    ang = 2.0 * np.pi * ((c[:, None] * c[None, :]) % FOURIER_GROUP) / FOURIER_GROUP
    return np.cos(ang), np.sin(ang)


def kernel(x, c, ctx, c_ctx, w_ada, b_ada, norm1_g, norm2_g, w_in, q_norm_g, k_norm_g, gmlp_v_g,
           w_spatial, b_spatial, w_out, w_mlp1, w_mlp2):
    depth = w_ada.shape[0]
    b, s, d = x.shape
    n_ctx = ctx.shape[1]
    gw = gmlp_v_g.shape[1]
    n_gmlp_heads = gw // GMLP_HEAD
    fw = d // 4
    qkv_w = w_in.shape[2] - fw - 2 * gw

    cond_rows = -(-(b + 1) // BF16_SUBLANES) * BF16_SUBLANES
    cond = jnp.zeros((cond_rows, d), F32).at[:b].set(c).at[b].set(c_ctx)
    ct_x, st_x = _rope_tables_t(s)
    ct_c = jnp.ones((HEAD_DIM, n_ctx), F32)
    st_c = jnp.zeros((HEAD_DIM, n_ctx), F32)
    cc, sc = _channel_dft()
    eye = np.eye(fw // FOURIER_GROUP)
    cs = jnp.asarray(np.concatenate([np.kron(eye, cc), np.kron(eye, sc)], axis=1), dtype=F32).astype(BF16)

    tm_x = 512 if s % 512 == 0 else KEY_CHUNK
    tm_in = 1024 if s % 1024 == 0 else tm_x
    tm_c = KEY_CHUNK

    for l in range(depth):
        mod = _adaln(cond, w_ada, b_ada, l).reshape(cond.shape[0], 6, 1, d)
        x_row = lambda bi: bi
        c_row = lambda bi: b

        wts = dict(g1=norm1_g[l].reshape(1, d), w_in=w_in, layer=l, qkv_w=qkv_w,
                   cs=cs, vg=gmlp_v_g[l].reshape(1, gw),
                   ws=w_spatial[l].transpose(1, 0, 2).reshape(CHUNK, n_gmlp_heads * CHUNK).astype(BF16),
                   bs=jnp.repeat(b_spatial[l].T, GMLP_HEAD, axis=1))
        rope_x = dict(rq=_gained_rope(ct_x, st_x, q_norm_g[l], Q_SCALE), rk=_gained_rope(ct_x, st_x, k_norm_g[l], 1.0))
        rope_c = dict(rq=_gained_rope(ct_c, st_c, q_norm_g[l], Q_SCALE), rk=_gained_rope(ct_c, st_c, k_norm_g[l], 1.0))
        q_x, k_x, v_x, g_x, gm_x = _in_proj(x, mod, x_row, tm=tm_in, **rope_x, **wts)
        q_c, k_c, v_c, g_c, gm_c = _in_proj(ctx, mod, c_row, tm=tm_c, **rope_c, **wts)

        logit_bound = (HEAD_DIM * Q_SCALE) * jnp.max(jnp.abs(q_norm_g[l])) * jnp.max(jnp.abs(k_norm_g[l]))
        a_x = _attention(q_x, (k_c, k_x), (v_c, v_x), logit_bound, tq=KEY_CHUNK)
        f_x = _fourier(g_x.reshape(b, 2 * s, fw))

        wo = w_out[l].astype(BF16)
        w1 = w_mlp1[l].astype(BF16)
        w2 = w_mlp2[l].astype(BF16)
        g2 = norm2_g[l].reshape(1, d)
        x = _out_mlp(x, a_x, f_x, gm_x, wo, g2, mod, x_row, w1, w2, tm=tm_in)

        if l < depth - 1:
            a_c = _attention(q_c, (k_c,), (v_c,), logit_bound, tq=KEY_CHUNK)
            f_c = _fourier(g_c.reshape(b, 2 * n_ctx, fw))
            ctx = _out_mlp(ctx, a_c, f_c, gm_c, wo, g2, mod, c_row, w1, w2, tm=tm_c)
    return x
```
